```python
import math
import jax, jax.numpy as jnp
from jax import lax
import numpy as np

D_MODEL = 1024
BATCH = 16
SEQ = 4096
DEPTH = 2

CHUNK = 64
HEAD_DIM = 64
EPS = 1e-6
A_WIDTH = D_MODEL // 2
A_HEADS = A_WIDTH // HEAD_DIM
A_LEFT_CHUNKS = 8
A_BAND = (A_LEFT_CHUNKS + 1) * CHUNK
A_MAX_REL = 128
B_WIDTH = D_MODEL - A_WIDTH
B_GROUP = 16
B_GROUPS = B_WIDTH // B_GROUP
B_STATE = 64
DT_MIN = 1e-3
DT_MAX = 1e-1
C_WIDTH = D_MODEL // 2
C_HEADS = 4
C_HEAD_DIM = C_WIDTH // C_HEADS
ROPE_BASE = 10000.0
D_WIDTH = D_MODEL - C_WIDTH
D_HEADS = D_WIDTH // HEAD_DIM
SB_BLOCK = 128
FF_HIDDEN = ((-(-8 * D_MODEL // 3)) + 255) // 256 * 256
N_EVEN = (DEPTH + 1) // 2
N_ODD = DEPTH // 2
EVEN_IN = 3 * A_WIDTH + B_WIDTH
ODD_IN = 4 * C_WIDTH + 3 * D_WIDTH

kernel_name = "hybrid_streaming_encoder_block"


def rms_norm(x, g):
    xf = x.astype(jnp.float32)
    y = xf * lax.rsqrt(jnp.mean(xf * xf, axis=-1, keepdims=True) + EPS)
    return (y * g.astype(jnp.float32)).astype(x.dtype)


def modulate(h, shift, scale):
    return h * (1.0 + scale[:, None, :]) + shift[:, None, :]


def swiglu(h, w_in, w_out):
    gate, up = jnp.split(h @ w_in, 2, axis=-1)
    return (jax.nn.silu(gate) * up) @ w_out


def rotary(x, pos):
    half = x.shape[-1] // 2
    freqs = ROPE_BASE ** (-jnp.arange(half, dtype=jnp.float32) / half)
    ang = pos.astype(jnp.float32)[:, None] * freqs[None, :]
    cos = jnp.cos(ang)[None, :, None, :]
    sin = jnp.sin(ang)[None, :, None, :]
    x1, x2 = x[..., :half], x[..., half:]
    return jnp.concatenate([x1 * cos - x2 * sin, x1 * sin + x2 * cos], axis=-1)


def chunked_relpos_attention(q, k, v, q_gain, k_gain, rel_bias):
    bsz, L, H, dh = q.shape
    n_chunks = L // CHUNK
    pad = A_LEFT_CHUNKS * CHUNK
    q = rms_norm(q, q_gain)
    k = rms_norm(k, k_gain)
    kp = jnp.pad(k, ((0, 0), (pad, 0), (0, 0), (0, 0)))
    vp = jnp.pad(v, ((0, 0), (pad, 0), (0, 0), (0, 0)))
    qc = q.reshape(bsz, n_chunks, CHUNK, H, dh)
    rel = (pad + jnp.arange(CHUNK))[:, None] - jnp.arange(A_BAND)[None, :]
    idx = jnp.clip(rel, -A_MAX_REL, A_MAX_REL) + A_MAX_REL
    bias = rel_bias.astype(jnp.float32)[:, idx]
    scale = dh ** -0.5

    def one_chunk(ci):
        kb = lax.dynamic_slice_in_dim(kp, ci * CHUNK, A_BAND, axis=1)
        vb = lax.dynamic_slice_in_dim(vp, ci * CHUNK, A_BAND, axis=1)
        qb = lax.dynamic_index_in_dim(qc, ci, axis=1, keepdims=False)
        s = jnp.einsum('bqhd,bkhd->bhqk', qb, kb).astype(jnp.float32) * scale + bias
        valid = (ci * CHUNK - pad + jnp.arange(A_BAND)) >= 0
        s = jnp.where(valid[None, None, None, :], s, -1e30)
        p = jax.nn.softmax(s, axis=-1).astype(vb.dtype)
        return jnp.einsum('bhqk,bkhd->bqhd', p, vb)

    out = lax.map(one_chunk, jnp.arange(n_chunks))
    return out.transpose(1, 0, 2, 3, 4).reshape(bsz, L, H * dh)


def s5_mixer(u, lam_re, lam_im, log_dt, b_re, b_im, c_re, c_im, d_skip, w_glu, b_glu):
    f32 = jnp.float32
    bsz, L, _ = u.shape
    ug = u.astype(f32).reshape(bsz, L, B_GROUPS, B_GROUP)
    lam_re = lam_re.astype(f32)
    lam_im = lam_im.astype(f32)
    b_re = b_re.astype(f32)
    b_im = b_im.astype(f32)
    dt = jnp.exp(log_dt.astype(f32))[:, None]
    mag = jnp.exp(lam_re * dt)
    ang = lam_im * dt
    abar_re = mag * jnp.cos(ang)
    abar_im = mag * jnp.sin(ang)
    den = lam_re * lam_re + lam_im * lam_im
    f_re = ((abar_re - 1.0) * lam_re + abar_im * lam_im) / den
    f_im = (abar_im * lam_re - (abar_re - 1.0) * lam_im) / den
    bbar_re = f_re[..., None] * b_re - f_im[..., None] * b_im
    bbar_im = f_re[..., None] * b_im + f_im[..., None] * b_re
    bu_re = jnp.einsum('blgh,gph->lbgp', ug, bbar_re)
    bu_im = jnp.einsum('blgh,gph->lbgp', ug, bbar_im)
    a_re = jnp.broadcast_to(abar_re[None, None], (L, 1, B_GROUPS, B_STATE))
    a_im = jnp.broadcast_to(abar_im[None, None], (L, 1, B_GROUPS, B_STATE))

    def combine(e1, e2):
        a1r, a1i, b1r, b1i = e1
        a2r, a2i, b2r, b2i = e2
        ar = a1r * a2r - a1i * a2i
        ai = a1r * a2i + a1i * a2r
        br = a2r * b1r - a2i * b1i + b2r
        bi = a2r * b1i + a2i * b1r + b2i
        return ar, ai, br, bi

    _, _, xr, xi = lax.associative_scan(combine, (a_re, a_im, bu_re, bu_im), axis=0)
    y = (jnp.einsum('lbgp,ghp->blgh', xr, c_re.astype(f32))
         - jnp.einsum('lbgp,ghp->blgh', xi, c_im.astype(f32))
         + d_skip.astype(f32) * ug)
    y = jax.nn.gelu(y.reshape(bsz, L, B_WIDTH))
    gate = jax.nn.sigmoid(y @ w_glu.astype(f32) + b_glu.astype(f32))
    return (y * gate).astype(u.dtype)


def retention_mixer(q, k, v, g, norm_g):
    f32 = jnp.float32
    bsz, L, _ = q.shape
    n_chunks = L // CHUNK
    pos = jnp.arange(L)
    shp = (bsz, L, C_HEADS, C_HEAD_DIM)
    q = rotary(q.astype(f32).reshape(shp), pos)
    k = rotary(k.astype(f32).reshape(shp), pos) * (C_HEAD_DIM ** -0.5)
    v = v.astype(f32).reshape(shp)
    log_gamma = jnp.log(1.0 - 2.0 ** (-5.0 - jnp.arange(C_HEADS, dtype=f32)))
    idx = jnp.arange(CHUNK, dtype=f32)
    intra_decay = jnp.exp(log_gamma[:, None, None] * jnp.abs(idx[:, None] - idx[None, :]))
    q_decay = jnp.exp(log_gamma[:, None] * (idx + 1.0))
    k_decay = jnp.exp(log_gamma[:, None] * (CHUNK - 1.0 - idx))
    chunk_decay = jnp.exp(log_gamma * CHUNK)

    def to_chunks(t):
        return t.reshape(bsz, n_chunks, CHUNK, C_HEADS, C_HEAD_DIM).transpose(1, 0, 3, 2, 4)

    def step(state, inp):
        qn, kn, vn = inp
        scores = jnp.einsum('bhid,bhjd->bhij', qn, kn) * intra_decay
        out = (jnp.einsum('bhij,bhjd->bhid', scores, vn)
               + jnp.einsum('bhid,bhde->bhie', qn * q_decay[:, :, None], state))
        state = (state * chunk_decay[:, None, None]
                 + jnp.einsum('bhjd,bhje->bhde', kn * k_decay[:, :, None], vn))
        return state, out

    state0 = jnp.zeros((bsz, C_HEADS, C_HEAD_DIM, C_HEAD_DIM), f32)
    _, o = lax.scan(step, state0, (to_chunks(q), to_chunks(k), to_chunks(v)))
    o = o.transpose(1, 0, 3, 2, 4).reshape(shp)
    mu = jnp.mean(o, axis=-1, keepdims=True)
    var = jnp.mean(jnp.square(o - mu), axis=-1, keepdims=True)
    o = ((o - mu) * lax.rsqrt(var + EPS)).reshape(bsz, L, C_WIDTH) * norm_g.astype(f32)
    return (jax.nn.silu(g.astype(f32)) * o).astype(g.dtype)


def stick_breaking_mixer(q, k, v):
    bsz, L, _ = q.shape
    shp = (bsz, L, D_HEADS, HEAD_DIM)
    q, k, v = q.reshape(shp), k.reshape(shp), v.reshape(shp)
    scale = HEAD_DIM ** -0.5
    outs = []
    for blk in range(L // SB_BLOCK):
        t0 = blk * SB_BLOCK
        t1 = t0 + SB_BLOCK
        z = jnp.einsum('bqhd,bkhd->bhqk', q[:, t0:t1], k[:, :t1]).astype(jnp.float32) * scale
        causal = jnp.arange(t1)[None, :] < (t0 + jnp.arange(SB_BLOCK))[:, None]
        log_1m_beta = jnp.where(causal, -jax.nn.softplus(z), 0.0)
        between = lax.cumsum(log_1m_beta, axis=3, reverse=True) - log_1m_beta
        w = jnp.where(causal, jnp.exp(jax.nn.log_sigmoid(z) + between), 0.0)
        outs.append(jnp.einsum('bhqk,bkhd->bqhd', w.astype(v.dtype), v[:, :t1]))
    return jnp.concatenate(outs, axis=1).reshape(bsz, L, D_WIDTH)


def even_mixer(h, w_in, w_out, q_gain, k_gain, rel_bias, lam_re, lam_im, log_dt,
               b_re, b_im, c_re, c_im, d_skip, w_glu, b_glu):
    bsz, L, _ = h.shape
    q, k, v, u = jnp.split(h @ w_in, [A_WIDTH, 2 * A_WIDTH, 3 * A_WIDTH], axis=-1)
    shp = (bsz, L, A_HEADS, HEAD_DIM)
    a_out = chunked_relpos_attention(q.reshape(shp), k.reshape(shp), v.reshape(shp),
                                     q_gain, k_gain, rel_bias)
    b_out = s5_mixer(u, lam_re, lam_im, log_dt, b_re, b_im, c_re, c_im, d_skip, w_glu, b_glu)
    return jnp.concatenate([a_out, b_out], axis=-1) @ w_out


def odd_mixer(h, w_in, w_out, norm_g):
    offs = [C_WIDTH, 2 * C_WIDTH, 3 * C_WIDTH, 4 * C_WIDTH,
            4 * C_WIDTH + D_WIDTH, 4 * C_WIDTH + 2 * D_WIDTH]
    cq, ck, cv, cg, dq, dk, dv = jnp.split(h @ w_in, offs, axis=-1)
    c_out = retention_mixer(cq, ck, cv, cg, norm_g)
    d_out = stick_breaking_mixer(dq, dk, dv)
    return jnp.concatenate([c_out, d_out], axis=-1) @ w_out


def setup_inputs(seed: int = 0) -> dict:
    key = jax.random.key(seed)
    ks = jax.random.split(key, 26)
    f32 = jnp.float32
    D = D_MODEL

    def nrm(k, shape, s):
        return jax.random.normal(k, shape, f32) * s

    return {
        "x": nrm(ks[0], (BATCH, SEQ, D), 1.0),
        "c": nrm(ks[1], (BATCH, D), 1.0),
        "ada_w": nrm(ks[2], (DEPTH, D, 6 * D), 0.5 * D ** -0.5),
        "ada_b": nrm(ks[3], (DEPTH, 6 * D), 0.01),
        "ln_mix_g": 1.0 + nrm(ks[4], (DEPTH, D), 0.01),
        "ln_ffn_g": 1.0 + nrm(ks[5], (DEPTH, D), 0.01),
        "ffn_w_in": nrm(ks[6], (DEPTH, D, 2 * FF_HIDDEN), D ** -0.5),
        "ffn_w_out": nrm(ks[7], (DEPTH, FF_HIDDEN, D), FF_HIDDEN ** -0.5),
        "ab_w_in": nrm(ks[8], (N_EVEN, D, EVEN_IN), D ** -0.5),
        "ab_w_out": nrm(ks[9], (N_EVEN, A_WIDTH + B_WIDTH, D), (A_WIDTH + B_WIDTH) ** -0.5),
        "a_q_gain": 1.0 + nrm(ks[10], (N_EVEN, HEAD_DIM), 0.01),
        "a_k_gain": 1.0 + nrm(ks[11], (N_EVEN, HEAD_DIM), 0.01),
        "a_rel_bias": nrm(ks[12], (N_EVEN, A_HEADS, 2 * A_MAX_REL + 1), 0.5),
        "s5_lambda_re": -0.5 + nrm(ks[13], (N_EVEN, B_GROUPS, B_STATE), 0.01),
        "s5_lambda_im": math.pi * jnp.arange(B_STATE, dtype=f32) + nrm(ks[14], (N_EVEN, B_GROUPS, B_STATE), 0.01),
        "s5_log_dt": jax.random.uniform(ks[15], (N_EVEN, B_GROUPS), f32, math.log(DT_MIN), math.log(DT_MAX)),
        "s5_b_re": nrm(ks[16], (N_EVEN, B_GROUPS, B_STATE, B_GROUP), (2 * B_GROUP) ** -0.5),
        "s5_b_im": nrm(ks[17], (N_EVEN, B_GROUPS, B_STATE, B_GROUP), (2 * B_GROUP) ** -0.5),
        "s5_c_re": nrm(ks[18], (N_EVEN, B_GROUPS, B_GROUP, B_STATE), 2.0 * B_STATE ** -0.5),
        "s5_c_im": nrm(ks[19], (N_EVEN, B_GROUPS, B_GROUP, B_STATE), 2.0 * B_STATE ** -0.5),
        "s5_d": nrm(ks[20], (N_EVEN, B_GROUPS, B_GROUP), 0.5),
        "s5_w_glu": nrm(ks[21], (N_EVEN, B_WIDTH, B_WIDTH), B_WIDTH ** -0.5),
        "s5_b_glu": nrm(ks[22], (N_EVEN, B_WIDTH), 0.01),
        "cd_w_in": nrm(ks[23], (N_ODD, D, ODD_IN), D ** -0.5),
        "cd_w_out": nrm(ks[24], (N_ODD, C_WIDTH + D_WIDTH, D), (C_WIDTH + D_WIDTH) ** -0.5),
        "ret_norm_g": 1.0 + nrm(ks[25], (N_ODD, C_WIDTH), 0.01),
    }


def reference(x, c, ada_w, ada_b, ln_mix_g, ln_ffn_g, ffn_w_in, ffn_w_out,
              ab_w_in, ab_w_out, a_q_gain, a_k_gain, a_rel_bias,
              s5_lambda_re, s5_lambda_im, s5_log_dt, s5_b_re, s5_b_im, s5_c_re, s5_c_im,
              s5_d, s5_w_glu, s5_b_glu, cd_w_in, cd_w_out, ret_norm_g):
    cond = jax.nn.silu(c)
    for layer in range(DEPTH):
        mod = cond @ ada_w[layer] + ada_b[layer]
        sh1, sc1, g1, sh2, sc2, g2 = jnp.split(mod, 6, axis=-1)
        h = modulate(rms_norm(x, ln_mix_g[layer]), sh1, sc1)
        i = layer // 2
        if layer % 2 == 0:
            mix = even_mixer(h, ab_w_in[i], ab_w_out[i], a_q_gain[i], a_k_gain[i], a_rel_bias[i],
                             s5_lambda_re[i], s5_lambda_im[i], s5_log_dt[i], s5_b_re[i], s5_b_im[i],
                             s5_c_re[i], s5_c_im[i], s5_d[i], s5_w_glu[i], s5_b_glu[i])
        else:
            mix = odd_mixer(h, cd_w_in[i], cd_w_out[i], ret_norm_g[i])
        x = x + g1[:, None, :] * mix
        h = modulate(rms_norm(x, ln_ffn_g[layer]), sh2, sc2)
        x = x + g2[:, None, :] * swiglu(h, ffn_w_in[layer], ffn_w_out[layer])
    return x
```

```python
import functools
import math

import jax
import jax.numpy as jnp
from jax import lax
from jax.experimental import pallas as pl
from jax.experimental.pallas import tpu as pltpu

F32 = jnp.float32
BF16 = jnp.bfloat16

D_MODEL = 1024
CHUNK = 64
HEAD_DIM = 64
EPS = 1e-6
A_WIDTH = 512
A_HEADS = 8
A_LEFT_CHUNKS = 8
A_BAND = (A_LEFT_CHUNKS + 1) * CHUNK
A_PAD = A_LEFT_CHUNKS * CHUNK
A_MAX_REL = 128
B_WIDTH = 512
B_GROUP = 16
B_GROUPS = 32
B_STATE = 64
C_WIDTH = 512
C_HEADS = 4
C_HEAD_DIM = 128
ROPE_BASE = 10000.0
D_WIDTH = 512
D_HEADS = 8
SB_BLOCK = 128
FF_HIDDEN = 2816
WIDTH = 512
LANES = 128
NEG_BIG = -1e30

VMEM_LIMIT = 56 * 1024 * 1024


def _params(*sem):
    return pltpu.CompilerParams(dimension_semantics=sem, vmem_limit_bytes=VMEM_LIMIT)


def _sigmoid(x):
    return 1.0 / (1.0 + jnp.exp(-x))


def _softplus(x):
    return jnp.maximum(x, 0.0) + jnp.log1p(jnp.exp(-jnp.abs(x)))


def _gelu_tanh(x):
    return 0.5 * x * (1.0 + jnp.tanh(math.sqrt(2.0 / math.pi) * (x + 0.044715 * (x * x * x))))


def _norm_mod(x, g, shift, scale):
    ms = jnp.mean(x * x, axis=-1, keepdims=True)
    y = x * lax.rsqrt(ms + EPS) * g
    return y * (1.0 + scale) + shift


def _dot(a, b):
    return jnp.dot(a, b, preferred_element_type=F32)


def _dot_nt(a, b):
    return lax.dot_general(a, b, (((1,), (1,)), ((), ())), preferred_element_type=F32)


def _ada_body(c_ref, w_ref, b_ref, o_ref):
    c = c_ref[...]
    cond = c * _sigmoid(c)
    o_ref[0] = jnp.dot(cond, w_ref[0], preferred_element_type=F32,
                       precision=lax.Precision.HIGHEST) + b_ref[0]


def ada_modulation(c, ada_w, ada_b):
    depth, d, n = ada_w.shape
    bsz = c.shape[0]
    tn = 1536
    return pl.pallas_call(
        _ada_body,
        grid=(depth, n // tn),
        in_specs=[pl.BlockSpec((bsz, d), lambda l, j: (0, 0)),
                  pl.BlockSpec((1, d, tn), lambda l, j: (l, 0, j)),
                  pl.BlockSpec((1, 1, tn), lambda l, j: (l, 0, j))],
        out_specs=pl.BlockSpec((1, bsz, tn), lambda l, j: (l, 0, j)),
        out_shape=jax.ShapeDtypeStruct((depth, bsz, n), F32),
        compiler_params=_params("arbitrary", "arbitrary"),
        name="ada_modulation",
    )(c, ada_w, ada_b.reshape(depth, 1, n))


def _proj_in0_body(x_ref, g_ref, sh_ref, sc_ref, w_ref, qg_ref, kg_ref, bd_ref,
                   q_ref, k_ref, v_ref, u_ref):
    h = _norm_mod(x_ref[0], g_ref[...], sh_ref[0], sc_ref[0]).astype(BF16)

    def head_norm(t, gain):
        ms = _dot((t * t).astype(BF16), bd_ref[...])
        return t * lax.rsqrt(ms + EPS) * gain

    q_ref[0] = head_norm(_dot(h, w_ref[:, 0:WIDTH]), qg_ref[...]).astype(BF16)
    k_ref[0] = head_norm(_dot(h, w_ref[:, WIDTH:2 * WIDTH]), kg_ref[...]).astype(BF16)
    v_ref[0] = _dot(h, w_ref[:, 2 * WIDTH:3 * WIDTH]).astype(BF16)
    u_ref[0] = _dot(h, w_ref[:, 3 * WIDTH:4 * WIDTH]).astype(BF16)


def proj_in0(x, ln_g, shift, scale, w_in, q_gain, k_gain, tm=512):
    bsz, L, d = x.shape
    n = w_in.shape[1]
    head_of = jnp.arange(WIDTH) // HEAD_DIM
    bd = (head_of[:, None] == head_of[None, :]).astype(BF16) * (1.0 / HEAD_DIM)
    qg = (jnp.tile(q_gain.astype(F32), A_HEADS) * (HEAD_DIM ** -0.5)).reshape(1, WIDTH)
    kg = jnp.tile(k_gain.astype(F32), A_HEADS).reshape(1, WIDTH)
    row = lambda b, i: (b, i, 0)
    const2 = lambda b, i: (0, 0)
    per_b = lambda b, i: (b, 0, 0)
    out = jax.ShapeDtypeStruct((bsz, L, WIDTH), BF16)
    return pl.pallas_call(
        _proj_in0_body,
        grid=(bsz, L // tm),
        in_specs=[pl.BlockSpec((1, tm, d), row),
                  pl.BlockSpec((1, d), const2),
                  pl.BlockSpec((1, 1, d), per_b),
                  pl.BlockSpec((1, 1, d), per_b),
                  pl.BlockSpec((d, n), const2),
                  pl.BlockSpec((1, WIDTH), const2),
                  pl.BlockSpec((1, WIDTH), const2),
                  pl.BlockSpec((WIDTH, WIDTH), const2)],
        out_specs=[pl.BlockSpec((1, tm, WIDTH), row)] * 4,
        out_shape=[out] * 4,
        compiler_params=_params("arbitrary", "arbitrary"),
        name="proj_in0",
    )(x, ln_g.reshape(1, d), shift, scale, w_in.astype(BF16), qg, kg, bd)


def _attn_a_body(q_ref, k_ref, v_ref, bias_ref, o_ref, *, tq):
    j = pl.program_id(1)
    win = tq + A_PAD
    start = pl.multiple_of(j * tq, tq)
    col = lax.broadcasted_iota(jnp.int32, (tq, win), 1)
    valid = col >= A_PAD - j * tq
    lane = lax.broadcasted_iota(jnp.int32, (tq, LANES), 1)
    low = lane < HEAD_DIM
    for hp in range(A_HEADS // 2):
        sl = slice(hp * LANES, (hp + 1) * LANES)
        qp = q_ref[0, :, sl]
        kw = k_ref[0, pl.ds(start, win), sl]
        vw = v_ref[0, pl.ds(start, win), sl]
        outs = []
        for e in range(2):
            qm = jnp.where(low if e == 0 else jnp.logical_not(low), qp, jnp.zeros_like(qp))
            s = _dot_nt(qm, kw) + bias_ref[2 * hp + e]
            s = jnp.where(valid, s, NEG_BIG)
            m = jnp.max(s, axis=-1, keepdims=True)
            p = jnp.exp(s - m)
            l = jnp.sum(p, axis=-1, keepdims=True)
            outs.append(_dot(p.astype(BF16), vw) / l)
        o_ref[0, :, sl] = jnp.where(low, outs[0], outs[1]).astype(BF16)


def attn_a_bias_table(rel_bias, tq):
    win = tq + A_PAD
    qi = jnp.arange(tq)
    kj = jnp.arange(win)
    band = kj[None, :] - (qi[:, None] // CHUNK) * CHUNK
    inside = (band >= 0) & (band < A_BAND)
    rel = (A_PAD + qi[:, None] % CHUNK) - band
    idx = jnp.clip(rel, -A_MAX_REL, A_MAX_REL) + A_MAX_REL
    tab = rel_bias.astype(F32)[:, idx]
    return jnp.where(inside[None], tab, NEG_BIG)


def attn_a(q, k, v, rel_bias, tq=128):
    bsz, L, _ = q.shape
    win = tq + A_PAD
    kp = jnp.pad(k, ((0, 0), (A_PAD, 0), (0, 0)))
    vp = jnp.pad(v, ((0, 0), (A_PAD, 0), (0, 0)))
    bias = attn_a_bias_table(rel_bias, tq)
    per_b = lambda b, i: (b, 0, 0)
    return pl.pallas_call(
        functools.partial(_attn_a_body, tq=tq),
        grid=(bsz, L // tq),
        in_specs=[pl.BlockSpec((1, tq, WIDTH), lambda b, i: (b, i, 0)),
                  pl.BlockSpec((1, L + A_PAD, WIDTH), per_b),
                  pl.BlockSpec((1, L + A_PAD, WIDTH), per_b),
                  pl.BlockSpec((A_HEADS, tq, win), lambda b, i: (0, 0, 0))],
        out_specs=pl.BlockSpec((1, tq, WIDTH), lambda b, i: (b, i, 0)),
        out_shape=jax.ShapeDtypeStruct((bsz, L, WIDTH), BF16),
        compiler_params=_params("arbitrary", "arbitrary"),
        name="attn_a",
    )(q, kp, vp, bias)


def s5_operators(lam_re, lam_im, log_dt, b_re, b_im, c_re, c_im, d_skip):
    f32 = F32
    hp = lax.Precision.HIGHEST
    lam_re, lam_im = lam_re.astype(f32), lam_im.astype(f32)
    b_re, b_im = b_re.astype(f32), b_im.astype(f32)
    c_re, c_im = c_re.astype(f32), c_im.astype(f32)
    dt = jnp.exp(log_dt.astype(f32))[:, None]
    mag = jnp.exp(lam_re * dt)
    ang = lam_im * dt
    abar_re = mag * jnp.cos(ang)
    abar_im = mag * jnp.sin(ang)
    den = lam_re * lam_re + lam_im * lam_im
    f_re = ((abar_re - 1.0) * lam_re + abar_im * lam_im) / den
    f_im = (abar_im * lam_re - (abar_re - 1.0) * lam_im) / den
    bbar_re = f_re[..., None] * b_re - f_im[..., None] * b_im
    bbar_im = f_re[..., None] * b_im + f_im[..., None] * b_re
    pw_re = jnp.ones(abar_re.shape + (1,), f32)
    pw_im = jnp.zeros(abar_re.shape + (1,), f32)
    cur_re, cur_im = abar_re[..., None], abar_im[..., None]
    while pw_re.shape[-1] < 2 * CHUNK:
        nr = pw_re * cur_re - pw_im * cur_im
        ni = pw_re * cur_im + pw_im * cur_re
        pw_re = jnp.concatenate([pw_re, nr], axis=-1)
        pw_im = jnp.concatenate([pw_im, ni], axis=-1)
        cur_re, cur_im = cur_re * cur_re - cur_im * cur_im, 2.0 * cur_re * cur_im
    ca_re = c_re[:, None] * pw_re.transpose(0, 2, 1)[:, :, None, :] - c_im[:, None] * pw_im.transpose(0, 2, 1)[:, :, None, :]
    ca_im = c_re[:, None] * pw_im.transpose(0, 2, 1)[:, :, None, :] + c_im[:, None] * pw_re.transpose(0, 2, 1)[:, :, None, :]
    kern = (jnp.einsum('gnop,gpi->gnoi', ca_re, bbar_re, precision=hp)
            - jnp.einsum('gnop,gpi->gnoi', ca_im, bbar_im, precision=hp))
    eye = jnp.eye(B_GROUP, dtype=f32)
    kern = kern.at[:, 0].add(d_skip.astype(f32)[:, :, None] * eye[None])
    s_idx = jnp.arange(CHUNK)
    lag = s_idx[None, :] - s_idx[:, None]
    m = kern[:, jnp.clip(lag, 0, CHUNK - 1)]
    m = jnp.where((lag >= 0)[None, :, :, None, None], m, 0.0)
    m = m.transpose(0, 1, 4, 2, 3).reshape(B_GROUPS, CHUNK * B_GROUP, CHUNK * B_GROUP)
    rev_re = pw_re[..., :CHUNK][..., ::-1]
    rev_im = pw_im[..., :CHUNK][..., ::-1]
    v_re = jnp.einsum('gps,gpi->gsip', rev_re, bbar_re) - jnp.einsum('gps,gpi->gsip', rev_im, bbar_im)
    v_im = jnp.einsum('gps,gpi->gsip', rev_re, bbar_im) + jnp.einsum('gps,gpi->gsip', rev_im, bbar_re)
    v = jnp.concatenate([v_re, v_im], axis=-1).reshape(B_GROUPS, CHUNK * B_GROUP, 2 * B_STATE)
    w_re = ca_re[:, 1:CHUNK + 1].transpose(0, 3, 1, 2)
    w_im = -ca_im[:, 1:CHUNK + 1].transpose(0, 3, 1, 2)
    w = jnp.concatenate([w_re, w_im], axis=1).reshape(B_GROUPS, 2 * B_STATE, CHUNK * B_GROUP)
    a_re, a_im = pw_re[..., CHUNK], pw_im[..., CHUNK]
    a = jnp.stack([jnp.concatenate([a_re, a_re], -1), jnp.concatenate([-a_im, a_im], -1)], axis=1)
    return m.astype(BF16), v.astype(BF16), w.astype(BF16), a


def _s5_body(u_ref, m_ref, v_ref, w_ref, a_ref, y_ref, s_ref, xs_ref, *, nc, bsz):
    u = u_ref[0]
    s_ref[...] = _dot(u, v_ref[0])
    a1 = a_ref[0, 0:1, :]
    a2 = a_ref[0, 1:2, :]

    def step(c, x):
        r = pl.multiple_of(c * bsz, bsz)
        xs_ref[pl.ds(r, bsz), :] = x
        return x * a1 + pltpu.roll(x, B_STATE, 1) * a2 + s_ref[pl.ds(r, bsz), :]

    lax.fori_loop(0, nc, step, jnp.zeros((bsz, 2 * B_STATE), F32))
    y = _dot(u, m_ref[0]) + _dot(xs_ref[...].astype(BF16), w_ref[0])
    y_ref[0] = _gelu_tanh(y).astype(BF16)


def s5_mixer_gelu(u, ops):
    m, v, w, a = ops
    bsz, L, _ = u.shape
    nc = L // CHUNK
    n = nc * bsz
    kdim = CHUNK * B_GROUP
    ug = u.reshape(bsz, nc, CHUNK, B_GROUPS, B_GROUP).transpose(3, 1, 0, 2, 4).reshape(B_GROUPS, n, kdim)
    per_g = lambda g: (g, 0, 0)
    yg = pl.pallas_call(
        functools.partial(_s5_body, nc=nc, bsz=bsz),
        grid=(B_GROUPS,),
        in_specs=[pl.BlockSpec((1, n, kdim), per_g),
                  pl.BlockSpec((1, kdim, kdim), per_g),
                  pl.BlockSpec((1, kdim, 2 * B_STATE), per_g),
                  pl.BlockSpec((1, 2 * B_STATE, kdim), per_g),
                  pl.BlockSpec((1, 2, 2 * B_STATE), per_g)],
        out_specs=pl.BlockSpec((1, n, kdim), per_g),
        out_shape=jax.ShapeDtypeStruct((B_GROUPS, n, kdim), BF16),
        scratch_shapes=[pltpu.VMEM((n, 2 * B_STATE), F32), pltpu.VMEM((n, 2 * B_STATE), F32)],
        compiler_params=_params("arbitrary"),
        name="s5_mixer",
    )(ug, m, v, w, a)
    return yg.reshape(B_GROUPS, nc, bsz, CHUNK, B_GROUP).transpose(2, 1, 3, 0, 4).reshape(bsz, L, B_WIDTH)


def _proj_out0_body(x_ref, a_ref, y_ref, wglu_ref, bglu_ref, wo_ref, g_ref, o_ref):
    y = y_ref[0]
    gate = _sigmoid(_dot(y, wglu_ref[...]) + bglu_ref[...])
    b_out = (y.astype(F32) * gate).astype(BF16)
    mix = _dot(a_ref[0], wo_ref[0:WIDTH, :]) + _dot(b_out, wo_ref[WIDTH:2 * WIDTH, :])
    o_ref[0] = x_ref[0] + g_ref[0] * mix


def proj_out0(x, a_out, y_gelu, w_glu, b_glu, w_out, gate, tm=512):
    bsz, L, d = x.shape
    row = lambda b, i: (b, i, 0)
    const2 = lambda b, i: (0, 0)
    return pl.pallas_call(
        _proj_out0_body,
        grid=(bsz, L // tm),
        in_specs=[pl.BlockSpec((1, tm, d), row),
                  pl.BlockSpec((1, tm, WIDTH), row),
                  pl.BlockSpec((1, tm, WIDTH), row),
                  pl.BlockSpec((WIDTH, WIDTH), const2),
                  pl.BlockSpec((1, WIDTH), const2),
                  pl.BlockSpec((2 * WIDTH, d), const2),
                  pl.BlockSpec((1, 1, d), lambda b, i: (b, 0, 0))],
        out_specs=pl.BlockSpec((1, tm, d), row),
        out_shape=jax.ShapeDtypeStruct((bsz, L, d), F32),
        compiler_params=_params("arbitrary", "arbitrary"),
        name="proj_out0",
    )(x, a_out, y_gelu, w_glu.astype(BF16), b_glu.astype(F32).reshape(1, WIDTH), w_out.astype(BF16), gate)


def _proj_out1_body(x_ref, c_ref, dd_ref, wo_ref, g_ref, o_ref):
    mix = _dot(c_ref[0], wo_ref[0:WIDTH, :]) + _dot(dd_ref[0], wo_ref[WIDTH:2 * WIDTH, :])
    o_ref[0] = x_ref[0] + g_ref[0] * mix


def proj_out1(x, c_out, d_out, w_out, gate, tm=512):
    bsz, L, d = x.shape
    row = lambda b, i: (b, i, 0)
    return pl.pallas_call(
        _proj_out1_body,
        grid=(bsz, L // tm),
        in_specs=[pl.BlockSpec((1, tm, d), row),
                  pl.BlockSpec((1, tm, WIDTH), row),
                  pl.BlockSpec((1, tm, WIDTH), row),
                  pl.BlockSpec((2 * WIDTH, d), lambda b, i: (0, 0)),
                  pl.BlockSpec((1, 1, d), lambda b, i: (b, 0, 0))],
        out_specs=pl.BlockSpec((1, tm, d), row),
        out_shape=jax.ShapeDtypeStruct((bsz, L, d), F32),
        compiler_params=_params("arbitrary", "arbitrary"),
        name="proj_out1",
    )(x, c_out, d_out, w_out.astype(BF16), gate)


FFN_CHUNK = 256


def _ffn_body(x_ref, g_ref, sh_ref, sc_ref, gate_ref, wg_ref, wu_ref, wo_ref, o_ref):
    x = x_ref[0]
    h = _norm_mod(x, g_ref[...], sh_ref[0], sc_ref[0]).astype(BF16)
    acc = jnp.zeros(x.shape, F32)
    for f in range(FF_HIDDEN // FFN_CHUNK):
        sl = slice(f * FFN_CHUNK, (f + 1) * FFN_CHUNK)
        gt = _dot(h, wg_ref[:, sl])
        up = _dot(h, wu_ref[:, sl])
        act = (gt * _sigmoid(gt) * up).astype(BF16)
        acc = acc + _dot(act, wo_ref[sl, :])
    o_ref[0] = x + gate_ref[0] * acc


def ffn(x, ln_g, shift, scale, gate, w_in, w_out, tm=512):
    bsz, L, d = x.shape
    row = lambda b, i: (b, i, 0)
    const2 = lambda b, i: (0, 0)
    per_b = lambda b, i: (b, 0, 0)
    w_in = w_in.astype(BF16)
    single = pl.Buffered(1)
    return pl.pallas_call(
        _ffn_body,
        grid=(bsz, L // tm),
        in_specs=[pl.BlockSpec((1, tm, d), row),
                  pl.BlockSpec((1, d), const2),
                  pl.BlockSpec((1, 1, d), per_b),
                  pl.BlockSpec((1, 1, d), per_b),
                  pl.BlockSpec((1, 1, d), per_b),
                  pl.BlockSpec((d, FF_HIDDEN), const2, pipeline_mode=single),
                  pl.BlockSpec((d, FF_HIDDEN), const2, pipeline_mode=single),
                  pl.BlockSpec((FF_HIDDEN, d), const2, pipeline_mode=single)],
        out_specs=pl.BlockSpec((1, tm, d), row),
        out_shape=jax.ShapeDtypeStruct((bsz, L, d), F32),
        compiler_params=_params("arbitrary", "arbitrary"),
        name="ffn",
    )(x, ln_g.reshape(1, d), shift, scale, gate, w_in[:, :FF_HIDDEN], w_in[:, FF_HIDDEN:], w_out.astype(BF16))


def _rotary(t, cos, sin_signed):
    halves = [pltpu.roll(t[:, h * C_HEAD_DIM:(h + 1) * C_HEAD_DIM], C_HEAD_DIM // 2, 1) for h in range(C_HEADS)]
    return t * cos + jnp.concatenate(halves, axis=1) * sin_signed


def _proj_in1_body(x_ref, g_ref, sh_ref, sc_ref, w_ref, cos_ref, sin_ref,
                   cq_ref, ck_ref, cv_ref, cg_ref, dq_ref, dk_ref, dv_ref):
    h = _norm_mod(x_ref[0], g_ref[...], sh_ref[0], sc_ref[0]).astype(BF16)
    proj = lambda j: _dot(h, w_ref[:, j * WIDTH:(j + 1) * WIDTH])
    cos, sin = cos_ref[...], sin_ref[...]
    cq_ref[0] = _rotary(proj(0), cos, sin).astype(BF16)
    ck_ref[0] = (_rotary(proj(1), cos, sin) * (C_HEAD_DIM ** -0.5)).astype(BF16)
    cv_ref[0] = proj(2).astype(BF16)
    cg_ref[0] = proj(3).astype(BF16)
    dq_ref[0] = (proj(4) * (HEAD_DIM ** -0.5)).astype(BF16)
    dk_ref[0] = proj(5).astype(BF16)
    dv_ref[0] = proj(6).astype(BF16)


def rope_tables(L):
    half = C_HEAD_DIM // 2
    freqs = ROPE_BASE ** (-jnp.arange(half, dtype=F32) / half)
    ang = jnp.arange(L).astype(F32)[:, None] * freqs[None, :]
    cos, sin = jnp.cos(ang), jnp.sin(ang)
    cos_t = jnp.tile(jnp.concatenate([cos, cos], axis=1), (1, C_HEADS))
    sin_t = jnp.tile(jnp.concatenate([-sin, sin], axis=1), (1, C_HEADS))
    return cos_t, sin_t


def proj_in1(x, ln_g, shift, scale, w_in, tm=512):
    bsz, L, d = x.shape
    n = w_in.shape[1]
    cos_t, sin_t = rope_tables(L)
    row = lambda b, i: (b, i, 0)
    const2 = lambda b, i: (0, 0)
    per_b = lambda b, i: (b, 0, 0)
    pos = lambda b, i: (i, 0)
    out = jax.ShapeDtypeStruct((bsz, L, WIDTH), BF16)
    return pl.pallas_call(
        _proj_in1_body,
        grid=(bsz, L // tm),
        in_specs=[pl.BlockSpec((1, tm, d), row),
                  pl.BlockSpec((1, d), const2),
                  pl.BlockSpec((1, 1, d), per_b),
                  pl.BlockSpec((1, 1, d), per_b),
                  pl.BlockSpec((d, n), const2),
                  pl.BlockSpec((tm, WIDTH), pos),
                  pl.BlockSpec((tm, WIDTH), pos)],
        out_specs=[pl.BlockSpec((1, tm, WIDTH), row)] * 7,
        out_shape=[out] * 7,
        compiler_params=_params("arbitrary", "arbitrary"),
        name="proj_in1",
    )(x, ln_g.reshape(1, d), shift, scale, w_in.astype(BF16), cos_t, sin_t)


def _retention_body(q_ref, k_ref, v_ref, g_ref, ng_ref, o_ref, *, nc):
    head = pl.program_id(1)
    hf = jnp.full((1, 1), head, jnp.int32).astype(F32)
    log_gamma = jnp.log(1.0 - jnp.exp2(-5.0 - hf))
    ri = lax.broadcasted_iota(jnp.int32, (CHUNK, CHUNK), 0).astype(F32)
    ci = lax.broadcasted_iota(jnp.int32, (CHUNK, CHUNK), 1).astype(F32)
    intra = jnp.exp(log_gamma * jnp.abs(ri - ci))
    idx = lax.broadcasted_iota(jnp.int32, (CHUNK, 1), 0).astype(F32)
    q_decay = jnp.exp(log_gamma * (idx + 1.0))
    k_decay = jnp.exp(log_gamma * (CHUNK - 1.0 - idx))
    chunk_decay = jnp.exp(log_gamma * CHUNK)
    ng = ng_ref[...]

    def step(c, state):
        r = pl.multiple_of(c * CHUNK, CHUNK)
        q = q_ref[0, pl.ds(r, CHUNK), :]
        k = k_ref[0, pl.ds(r, CHUNK), :]
        v = v_ref[0, pl.ds(r, CHUNK), :]
        scores = _dot_nt(q, k) * intra
        o = _dot(scores.astype(BF16), v) + _dot((q.astype(F32) * q_decay).astype(BF16), state.astype(BF16))
        kd = (k.astype(F32) * k_decay).T.astype(BF16)
        state = state * chunk_decay + _dot(kd, v)
        mu = jnp.mean(o, axis=-1, keepdims=True)
        cen = o - mu
        var = jnp.mean(cen * cen, axis=-1, keepdims=True)
        on = cen * lax.rsqrt(var + EPS) * ng
        gt = g_ref[0, pl.ds(r, CHUNK), :].astype(F32)
        o_ref[0, pl.ds(r, CHUNK), :] = (gt * _sigmoid(gt) * on).astype(BF16)
        return state

    lax.fori_loop(0, nc, step, jnp.zeros((C_HEAD_DIM, C_HEAD_DIM), F32))


def retention(q, k, v, g, norm_g):
    bsz, L, _ = q.shape
    nc = L // CHUNK
    blk = pl.BlockSpec((1, L, C_HEAD_DIM), lambda b, h: (b, 0, h))
    return pl.pallas_call(
        functools.partial(_retention_body, nc=nc),
        grid=(bsz, C_HEADS),
        in_specs=[blk, blk, blk, blk, pl.BlockSpec((1, C_HEAD_DIM), lambda b, h: (0, h))],
        out_specs=blk,
        out_shape=jax.ShapeDtypeStruct((bsz, L, C_WIDTH), BF16),
        compiler_params=_params("arbitrary", "arbitrary"),
        name="retention",
    )(q, k, v, g, norm_g.astype(F32).reshape(1, C_WIDTH))


SB_SUPER = 4


def _stick_body(q_ref, k_ref, v_ref, tri_ref, o_ref):
    qb = pl.program_id(2)
    q0 = qb * SB_BLOCK
    qp = q_ref[0]
    lane = lax.broadcasted_iota(jnp.int32, (SB_BLOCK, LANES), 1)
    low = lane < HEAD_DIM
    zero = jnp.zeros_like(qp)
    qm = (jnp.where(low, qp, zero), jnp.where(low, zero, qp))
    row_pos = q0 + lax.broadcasted_iota(jnp.int32, (SB_BLOCK, SB_BLOCK), 0)
    col_idx = lax.broadcasted_iota(jnp.int32, (SB_BLOCK, SB_BLOCK), 1)
    tri = tri_ref[...]
    n_super = qb // SB_SUPER + 1

    def body(it, carry):
        sup = n_super - 1 - it
        out = []
        for e in range(2):
            acc, later = carry[e]
            for sub in range(SB_SUPER - 1, -1, -1):
                k0 = pl.multiple_of((sup * SB_SUPER + sub) * SB_BLOCK, SB_BLOCK)
                kb = k_ref[0, pl.ds(k0, SB_BLOCK), :]
                vb = v_ref[0, pl.ds(k0, SB_BLOCK), :]
                z = _dot_nt(qm[e], kb)
                causal = (k0 + col_idx) < row_pos
                sp = _softplus(z)
                lmb = jnp.where(causal, -sp, 0.0)
                hi = lmb.astype(BF16)
                lo = (lmb - hi.astype(F32)).astype(BF16)
                cs = _dot(hi, tri) + _dot(lo, tri)
                between = later + cs[:, 0:SB_BLOCK]
                w = jnp.where(causal, jnp.exp(z - sp + between), 0.0)
                acc = acc + _dot(w.astype(BF16), vb)
                later = later + cs[:, SB_BLOCK:SB_BLOCK + 1]
            out.append((acc, later))
        return tuple(out)

    init = tuple((jnp.zeros((SB_BLOCK, LANES), F32), jnp.zeros((SB_BLOCK, 1), F32)) for _ in range(2))
    res = lax.fori_loop(0, n_super, body, init)
    o_ref[0] = jnp.where(low, res[0][0], res[1][0]).astype(BF16)


def stick_breaking(q, k, v):
    bsz, L, _ = q.shape
    nq = L // SB_BLOCK
    assert nq % SB_SUPER == 0
    j = jnp.arange(SB_BLOCK)
    strict = (j[:, None] > j[None, :]).astype(BF16)
    tri = jnp.concatenate([strict, jnp.ones((SB_BLOCK, 1), BF16), jnp.zeros((SB_BLOCK, SB_BLOCK - 1), BF16)], axis=1)
    kv = pl.BlockSpec((1, L, LANES), lambda b, h, i: (b, 0, h))
    qo = pl.BlockSpec((1, SB_BLOCK, LANES), lambda b, h, i: (b, i, h))
    return pl.pallas_call(
        _stick_body,
        grid=(bsz, D_HEADS // 2, nq),
        in_specs=[qo, kv, kv, pl.BlockSpec((SB_BLOCK, 2 * SB_BLOCK), lambda b, h, i: (0, 0))],
        out_specs=qo,
        out_shape=jax.ShapeDtypeStruct((bsz, L, D_WIDTH), BF16),
        compiler_params=_params("arbitrary", "arbitrary", "arbitrary"),
        name="stick_breaking",
    )(q, k, v, tri)


def kernel(x, c, ada_w, ada_b, ln_mix_g, ln_ffn_g, ffn_w_in, ffn_w_out, ab_w_in, ab_w_out, a_q_gain, a_k_gain, a_rel_bias, s5_lambda_re, s5_lambda_im, s5_log_dt, s5_b_re, s5_b_im, s5_c_re, s5_c_im, s5_d, s5_w_glu, s5_b_glu, cd_w_in, cd_w_out, ret_norm_g):
    bsz, L, d = x.shape
    mod = ada_modulation(c, ada_w, ada_b)
    mods = [[mod[l, :, None, i * d:(i + 1) * d] for i in range(6)] for l in range(mod.shape[0])]

    sh1, sc1, g1, sh2, sc2, g2 = mods[0]
    q, k, v, u = proj_in0(x, ln_mix_g[0], sh1, sc1, ab_w_in[0], a_q_gain[0], a_k_gain[0])
    a_out = attn_a(q, k, v, a_rel_bias[0])
    ops = s5_operators(s5_lambda_re[0], s5_lambda_im[0], s5_log_dt[0], s5_b_re[0], s5_b_im[0],
                       s5_c_re[0], s5_c_im[0], s5_d[0])
    y_gelu = s5_mixer_gelu(u, ops)
    x = proj_out0(x, a_out, y_gelu, s5_w_glu[0], s5_b_glu[0], ab_w_out[0], g1)
    x = ffn(x, ln_ffn_g[0], sh2, sc2, g2, ffn_w_in[0], ffn_w_out[0])

    sh1, sc1, g1, sh2, sc2, g2 = mods[1]
    cq, ck, cv, cg, dq, dk, dv = proj_in1(x, ln_mix_g[1], sh1, sc1, cd_w_in[0])
    c_out = retention(cq, ck, cv, cg, ret_norm_g[0])
    d_out = stick_breaking(dq, dk, dv)
    x = proj_out1(x, c_out, d_out, cd_w_out[0], g1)
    x = ffn(x, ln_ffn_g[1], sh2, sc2, g2, ffn_w_in[1], ffn_w_out[1])
    return x
```

```python
import functools
import math

import jax
import jax.numpy as jnp
from jax import lax
from jax.experimental import pallas as pl
from jax.experimental.pallas import tpu as pltpu

F32 = jnp.float32
BF16 = jnp.bfloat16

D_MODEL = 1024
CHUNK = 64
HEAD_DIM = 64
EPS = 1e-6
A_WIDTH = 512
A_HEADS = 8
A_LEFT_CHUNKS = 8
A_BAND = (A_LEFT_CHUNKS + 1) * CHUNK
A_PAD = A_LEFT_CHUNKS * CHUNK
A_MAX_REL = 128
B_WIDTH = 512
B_GROUP = 16
B_GROUPS = 32
B_STATE = 64
C_WIDTH = 512
C_HEADS = 4
C_HEAD_DIM = 128
ROPE_BASE = 10000.0
D_WIDTH = 512
D_HEADS = 8
FF_HIDDEN = 2816
WIDTH = 512
LANES = 128
NEG_BIG = -1e30
A_Q_SCALE = HEAD_DIM ** -0.5 * math.log2(math.e)

VMEM_LIMIT = 56 * 1024 * 1024


def _params(*sem):
    return pltpu.CompilerParams(dimension_semantics=sem, vmem_limit_bytes=VMEM_LIMIT)


def _sigmoid(x):
    return 1.0 / (1.0 + jnp.exp(-x))


def _gelu_tanh(x):
    return 0.5 * x * (1.0 + jnp.tanh(math.sqrt(2.0 / math.pi) * (x + 0.044715 * (x * x * x))))


def _norm_mod(x, g, shift, scale):
    ms = jnp.mean(x * x, axis=-1, keepdims=True)
    y = x * lax.rsqrt(ms + EPS) * g
    return y * (1.0 + scale) + shift


def _dot(a, b):
    return jnp.dot(a, b, preferred_element_type=F32)


def _dot_nt(a, b):
    return lax.dot_general(a, b, (((1,), (1,)), ((), ())), preferred_element_type=F32)


def _ada_body(c_ref, w_ref, b_ref, o_ref):
    c = c_ref[...]
    cond = c * _sigmoid(c)
    o_ref[0] = jnp.dot(cond, w_ref[0], preferred_element_type=F32,
                       precision=lax.Precision.HIGHEST) + b_ref[0]


def ada_modulation(c, ada_w, ada_b):
    depth, d, n = ada_w.shape
    bsz = c.shape[0]
    tn = 1536
    return pl.pallas_call(
        _ada_body,
        grid=(depth, n // tn),
        in_specs=[pl.BlockSpec((bsz, d), lambda l, j: (0, 0)),
                  pl.BlockSpec((1, d, tn), lambda l, j: (l, 0, j)),
                  pl.BlockSpec((1, 1, tn), lambda l, j: (l, 0, j))],
        out_specs=pl.BlockSpec((1, bsz, tn), lambda l, j: (l, 0, j)),
        out_shape=jax.ShapeDtypeStruct((depth, bsz, n), F32),
        compiler_params=_params("arbitrary", "arbitrary"),
        name="ada_modulation",
    )(c, ada_w, ada_b.reshape(depth, 1, n))


def _proj_in0_body(x_ref, g_ref, sh_ref, sc_ref, w_ref, qg_ref, kg_ref, bd_ref, kz_ref, vz_ref,
                   q_ref, k_ref, v_ref, u_ref):
    del kz_ref, vz_ref
    h = _norm_mod(x_ref[0], g_ref[...], sh_ref[0], sc_ref[0]).astype(BF16)

    def head_norm(t, gain):
        ms = _dot((t * t).astype(BF16), bd_ref[...])
        return t * lax.rsqrt(ms + EPS) * gain

    q_ref[0] = head_norm(_dot(h, w_ref[:, 0:WIDTH]), qg_ref[...]).astype(BF16)
    k_ref[0] = head_norm(_dot(h, w_ref[:, WIDTH:2 * WIDTH]), kg_ref[...]).astype(BF16)
    v_ref[0] = _dot(h, w_ref[:, 2 * WIDTH:3 * WIDTH]).astype(BF16)
    u_ref[0] = _dot(h, w_ref[:, 3 * WIDTH:4 * WIDTH]).astype(BF16)


def proj_in0(x, ln_g, shift, scale, w_in, q_gain, k_gain, tm=512):
    bsz, L, d = x.shape
    n = w_in.shape[1]
    assert A_PAD % tm == 0
    head_of = jnp.arange(WIDTH) // HEAD_DIM
    bd = (head_of[:, None] == head_of[None, :]).astype(BF16) * (1.0 / HEAD_DIM)
    qg = (jnp.tile(q_gain.astype(F32), A_HEADS) * A_Q_SCALE).reshape(1, WIDTH)
    kg = jnp.tile(k_gain.astype(F32), A_HEADS).reshape(1, WIDTH)
    row = lambda b, i: (b, i, 0)
    padded_row = lambda b, i: (b, i + A_PAD // tm, 0)
    const2 = lambda b, i: (0, 0)
    per_b = lambda b, i: (b, 0, 0)
    out = jax.ShapeDtypeStruct((bsz, L, WIDTH), BF16)
    padded = jax.ShapeDtypeStruct((bsz, L + A_PAD, WIDTH), BF16)
    zeros = jnp.zeros(padded.shape, BF16)
    hbm = pl.BlockSpec(memory_space=pl.ANY)
    return pl.pallas_call(
        _proj_in0_body,
        grid=(bsz, L // tm),
        in_specs=[pl.BlockSpec((1, tm, d), row),
                  pl.BlockSpec((1, d), const2),
                  pl.BlockSpec((1, 1, d), per_b),
                  pl.BlockSpec((1, 1, d), per_b),
                  pl.BlockSpec((d, n), const2),
                  pl.BlockSpec((1, WIDTH), const2),
                  pl.BlockSpec((1, WIDTH), const2),
                  pl.BlockSpec((WIDTH, WIDTH), const2),
                  hbm, hbm],
        out_specs=[pl.BlockSpec((1, tm, WIDTH), row),
                   pl.BlockSpec((1, tm, WIDTH), padded_row),
                   pl.BlockSpec((1, tm, WIDTH), padded_row),
                   pl.BlockSpec((1, tm, WIDTH), row)],
        out_shape=[out, padded, padded, out],
        input_output_aliases={8: 1, 9: 2},
        compiler_params=_params("arbitrary", "arbitrary"),
        name="proj_in0",
    )(x, ln_g.reshape(1, d), shift, scale, w_in.astype(BF16), qg, kg, bd, zeros, zeros)


def _attn_a_body(q_ref, k_ref, v_ref, bias_ref, o_ref, *, tq):
    j = pl.program_id(1)
    win = tq + A_PAD
    start = pl.multiple_of(j * tq, tq)
    col = lax.broadcasted_iota(jnp.int32, (tq, win), 1)
    valid = col >= A_PAD - j * tq
    lane = lax.broadcasted_iota(jnp.int32, (tq, LANES), 1)
    low = lane < HEAD_DIM
    slices = [slice(hp * LANES, (hp + 1) * LANES) for hp in range(A_HEADS // 2)]
    vws = [v_ref[0, pl.ds(start, win), sl] for sl in slices]
    scores = []
    for hp, sl in enumerate(slices):
        qp = q_ref[0, :, sl]
        kw = k_ref[0, pl.ds(start, win), sl]
        zero = jnp.zeros_like(qp)
        for e in range(2):
            qm = jnp.where(low, qp, zero) if e == 0 else jnp.where(low, zero, qp)
            scores.append(jnp.where(valid, _dot_nt(qm, kw) + bias_ref[2 * hp + e], NEG_BIG))
    probs, inv_sums = [], []
    for s in scores:
        p = jnp.exp2(s - jnp.max(s, axis=-1, keepdims=True))
        inv_sums.append(1.0 / jnp.sum(p, axis=-1, keepdims=True))
        probs.append(p.astype(BF16))
    outs = [_dot(p, vws[h // 2]) * inv for h, (p, inv) in enumerate(zip(probs, inv_sums))]
    for hp, sl in enumerate(slices):
        o_ref[0, :, sl] = jnp.where(low, outs[2 * hp], outs[2 * hp + 1]).astype(BF16)


def attn_a_bias_table(rel_bias, tq):
    win = tq + A_PAD
    period = tq + win
    m = jnp.arange(period)
    diff = jnp.where(m < win, m, m - period)
    idx = jnp.clip(A_PAD - diff, -A_MAX_REL, A_MAX_REL) + A_MAX_REL
    x = rel_bias.astype(F32)[:, idx] * math.log2(math.e)
    tab = jnp.tile(x, (1, tq + 1))[:, :tq * (period - 1)].reshape(-1, tq, period - 1)[:, :, :win]
    qi = jnp.arange(tq)
    band = jnp.arange(win)[None, :] - (qi[:, None] // CHUNK) * CHUNK
    inside = (band >= 0) & (band < A_BAND)
    return jnp.where(inside[None], tab, NEG_BIG)


def attn_a(q, kp, vp, rel_bias, tq=256):
    bsz, L, _ = q.shape
    win = tq + A_PAD
    bias = attn_a_bias_table(rel_bias, tq)
    per_b = lambda b, i: (b, 0, 0)
    return pl.pallas_call(
        functools.partial(_attn_a_body, tq=tq),
        grid=(bsz, L // tq),
        in_specs=[pl.BlockSpec((1, tq, WIDTH), lambda b, i: (b, i, 0)),
                  pl.BlockSpec((1, L + A_PAD, WIDTH), per_b),
                  pl.BlockSpec((1, L + A_PAD, WIDTH), per_b),
                  pl.BlockSpec((A_HEADS, tq, win), lambda b, i: (0, 0, 0), pipeline_mode=pl.Buffered(1))],
        out_specs=pl.BlockSpec((1, tq, WIDTH), lambda b, i: (b, i, 0)),
        out_shape=jax.ShapeDtypeStruct((bsz, L, WIDTH), BF16),
        compiler_params=_params("arbitrary", "arbitrary"),
        name="attn_a",
    )(q, kp, vp, bias)


def s5_operators(lam_re, lam_im, log_dt, b_re, b_im, c_re, c_im, d_skip):
    f32 = F32
    hp = lax.Precision.HIGHEST
    lam_re, lam_im = lam_re.astype(f32), lam_im.astype(f32)
    b_re, b_im = b_re.astype(f32), b_im.astype(f32)
    c_re, c_im = c_re.astype(f32), c_im.astype(f32)
    dt = jnp.exp(log_dt.astype(f32))[:, None]
    mag = jnp.exp(lam_re * dt)
    ang = lam_im * dt
    abar_re = mag * jnp.cos(ang)
    abar_im = mag * jnp.sin(ang)
    den = lam_re * lam_re + lam_im * lam_im
    f_re = ((abar_re - 1.0) * lam_re + abar_im * lam_im) / den
    f_im = (abar_im * lam_re - (abar_re - 1.0) * lam_im) / den
    bbar_re = f_re[..., None] * b_re - f_im[..., None] * b_im
    bbar_im = f_re[..., None] * b_im + f_im[..., None] * b_re
    pw_re = jnp.ones(abar_re.shape + (1,), f32)
    pw_im = jnp.zeros(abar_re.shape + (1,), f32)
    cur_re, cur_im = abar_re[..., None], abar_im[..., None]
    while pw_re.shape[-1] < 2 * CHUNK:
        nr = pw_re * cur_re - pw_im * cur_im
        ni = pw_re * cur_im + pw_im * cur_re
        pw_re = jnp.concatenate([pw_re, nr], axis=-1)
        pw_im = jnp.concatenate([pw_im, ni], axis=-1)
        cur_re, cur_im = cur_re * cur_re - cur_im * cur_im, 2.0 * cur_re * cur_im
    ca_re = c_re[:, None] * pw_re.transpose(0, 2, 1)[:, :, None, :] - c_im[:, None] * pw_im.transpose(0, 2, 1)[:, :, None, :]
    ca_im = c_re[:, None] * pw_im.transpose(0, 2, 1)[:, :, None, :] + c_im[:, None] * pw_re.transpose(0, 2, 1)[:, :, None, :]
    kern = (jnp.einsum('gnop,gpi->gnoi', ca_re, bbar_re, precision=hp)
            - jnp.einsum('gnop,gpi->gnoi', ca_im, bbar_im, precision=hp))
    eye = jnp.eye(B_GROUP, dtype=f32)
    kern = kern.at[:, 0].add(d_skip.astype(f32)[:, :, None] * eye[None])
    s_idx = jnp.arange(CHUNK)
    lag = s_idx[None, :] - s_idx[:, None]
    m = kern[:, jnp.clip(lag, 0, CHUNK - 1)]
    m = jnp.where((lag >= 0)[None, :, :, None, None], m, 0.0)
    m = m.transpose(0, 1, 4, 2, 3).reshape(B_GROUPS, CHUNK * B_GROUP, CHUNK * B_GROUP)
    rev_re = pw_re[..., :CHUNK][..., ::-1]
    rev_im = pw_im[..., :CHUNK][..., ::-1]
    v_re = jnp.einsum('gps,gpi->gsip', rev_re, bbar_re) - jnp.einsum('gps,gpi->gsip', rev_im, bbar_im)
    v_im = jnp.einsum('gps,gpi->gsip', rev_re, bbar_im) + jnp.einsum('gps,gpi->gsip', rev_im, bbar_re)
    v = jnp.concatenate([v_re, v_im], axis=-1).reshape(B_GROUPS, CHUNK * B_GROUP, 2 * B_STATE)
    w_re = ca_re[:, 1:CHUNK + 1].transpose(0, 3, 1, 2)
    w_im = -ca_im[:, 1:CHUNK + 1].transpose(0, 3, 1, 2)
    w = jnp.concatenate([w_re, w_im], axis=1).reshape(B_GROUPS, 2 * B_STATE, CHUNK * B_GROUP)
    a_re, a_im = pw_re[..., CHUNK], pw_im[..., CHUNK]
    a = jnp.stack([jnp.concatenate([a_re, a_re], -1), jnp.concatenate([-a_im, a_im], -1)], axis=1)
    return m.astype(BF16), v.astype(BF16), w.astype(BF16), a


def _s5_body(u_ref, m_ref, v_ref, w_ref, a_ref, y_ref, s_ref, xs_ref, *, nc, bsz):
    u = u_ref[0]
    s_ref[...] = _dot(u, v_ref[0])
    a1 = a_ref[0, 0:1, :]
    a2 = a_ref[0, 1:2, :]

    def step(c, x):
        r = pl.multiple_of(c * bsz, bsz)
        xs_ref[pl.ds(r, bsz), :] = x
        return x * a1 + pltpu.roll(x, B_STATE, 1) * a2 + s_ref[pl.ds(r, bsz), :]

    lax.fori_loop(0, nc, step, jnp.zeros((bsz, 2 * B_STATE), F32))
    y = _dot(u, m_ref[0]) + _dot(xs_ref[...].astype(BF16), w_ref[0])
    y_ref[0] = _gelu_tanh(y).astype(BF16)


def s5_mixer_gelu(u, ops):
    m, v, w, a = ops
    bsz, L, _ = u.shape
    nc = L // CHUNK
    n = nc * bsz
    kdim = CHUNK * B_GROUP
    ug = u.reshape(bsz, nc, CHUNK, B_GROUPS, B_GROUP).transpose(3, 1, 0, 2, 4).reshape(B_GROUPS, n, kdim)
    per_g = lambda g: (g, 0, 0)
    yg = pl.pallas_call(
        functools.partial(_s5_body, nc=nc, bsz=bsz),
        grid=(B_GROUPS,),
        in_specs=[pl.BlockSpec((1, n, kdim), per_g),
                  pl.BlockSpec((1, kdim, kdim), per_g),
                  pl.BlockSpec((1, kdim, 2 * B_STATE), per_g),
                  pl.BlockSpec((1, 2 * B_STATE, kdim), per_g),
                  pl.BlockSpec((1, 2, 2 * B_STATE), per_g)],
        out_specs=pl.BlockSpec((1, n, kdim), per_g),
        out_shape=jax.ShapeDtypeStruct((B_GROUPS, n, kdim), BF16),
        scratch_shapes=[pltpu.VMEM((n, 2 * B_STATE), F32), pltpu.VMEM((n, 2 * B_STATE), F32)],
        compiler_params=_params("arbitrary"),
        name="s5_mixer",
    )(ug, m, v, w, a)
    return yg.reshape(B_GROUPS, nc, bsz, CHUNK, B_GROUP).transpose(2, 1, 3, 0, 4).reshape(bsz, L, B_WIDTH)


def _proj_out0_body(x_ref, a_ref, y_ref, wglu_ref, bglu_ref, wo_ref, g_ref, o_ref):
    y = y_ref[0]
    gate = _sigmoid(_dot(y, wglu_ref[...]) + bglu_ref[...])
    b_out = (y.astype(F32) * gate).astype(BF16)
    mix = _dot(a_ref[0], wo_ref[0:WIDTH, :]) + _dot(b_out, wo_ref[WIDTH:2 * WIDTH, :])
    o_ref[0] = x_ref[0] + g_ref[0] * mix


def proj_out0(x, a_out, y_gelu, w_glu, b_glu, w_out, gate, tm=512):
    bsz, L, d = x.shape
    row = lambda b, i: (b, i, 0)
    const2 = lambda b, i: (0, 0)
    return pl.pallas_call(
        _proj_out0_body,
        grid=(bsz, L // tm),
        in_specs=[pl.BlockSpec((1, tm, d), row),
                  pl.BlockSpec((1, tm, WIDTH), row),
                  pl.BlockSpec((1, tm, WIDTH), row),
                  pl.BlockSpec((WIDTH, WIDTH), const2),
                  pl.BlockSpec((1, WIDTH), const2),
                  pl.BlockSpec((2 * WIDTH, d), const2),
                  pl.BlockSpec((1, 1, d), lambda b, i: (b, 0, 0))],
        out_specs=pl.BlockSpec((1, tm, d), row),
        out_shape=jax.ShapeDtypeStruct((bsz, L, d), F32),
        compiler_params=_params("arbitrary", "arbitrary"),
        name="proj_out0",
    )(x, a_out, y_gelu, w_glu.astype(BF16), b_glu.astype(F32).reshape(1, WIDTH), w_out.astype(BF16), gate)


def _proj_out1_body(x_ref, c_ref, dd_ref, wo_ref, g_ref, o_ref):
    mix = _dot(c_ref[0], wo_ref[0:WIDTH, :]) + _dot(dd_ref[0], wo_ref[WIDTH:2 * WIDTH, :])
    o_ref[0] = x_ref[0] + g_ref[0] * mix


def proj_out1(x, c_out, d_out, w_out, gate, tm=512):
    bsz, L, d = x.shape
    row = lambda b, i: (b, i, 0)
    return pl.pallas_call(
        _proj_out1_body,
        grid=(bsz, L // tm),
        in_specs=[pl.BlockSpec((1, tm, d), row),
                  pl.BlockSpec((1, tm, WIDTH), row),
                  pl.BlockSpec((1, tm, WIDTH), row),
                  pl.BlockSpec((2 * WIDTH, d), lambda b, i: (0, 0)),
                  pl.BlockSpec((1, 1, d), lambda b, i: (b, 0, 0))],
        out_specs=pl.BlockSpec((1, tm, d), row),
        out_shape=jax.ShapeDtypeStruct((bsz, L, d), F32),
        compiler_params=_params("arbitrary", "arbitrary"),
        name="proj_out1",
    )(x, c_out, d_out, w_out.astype(BF16), gate)


FFN_CHUNK = 256


def _ffn_body(x_ref, g_ref, sh_ref, sc_ref, gate_ref, wg_ref, wu_ref, wo_ref, o_ref):
    x = x_ref[0]
    h = _norm_mod(x, g_ref[...], sh_ref[0], sc_ref[0]).astype(BF16)
    acc = jnp.zeros(x.shape, F32)
    for f in range(FF_HIDDEN // FFN_CHUNK):
        sl = slice(f * FFN_CHUNK, (f + 1) * FFN_CHUNK)
        gt = _dot(h, wg_ref[:, sl])
        up = _dot(h, wu_ref[:, sl])
        act = (gt * _sigmoid(gt) * up).astype(BF16)
        acc = acc + _dot(act, wo_ref[sl, :])
    o_ref[0] = x + gate_ref[0] * acc


def ffn(x, ln_g, shift, scale, gate, w_in, w_out, tm=512):
    bsz, L, d = x.shape
    row = lambda b, i: (b, i, 0)
    const2 = lambda b, i: (0, 0)
    per_b = lambda b, i: (b, 0, 0)
    w_in = w_in.astype(BF16)
    single = pl.Buffered(1)
    return pl.pallas_call(
        _ffn_body,
        grid=(bsz, L // tm),
        in_specs=[pl.BlockSpec((1, tm, d), row),
                  pl.BlockSpec((1, d), const2),
                  pl.BlockSpec((1, 1, d), per_b),
                  pl.BlockSpec((1, 1, d), per_b),
                  pl.BlockSpec((1, 1, d), per_b),
                  pl.BlockSpec((d, FF_HIDDEN), const2, pipeline_mode=single),
                  pl.BlockSpec((d, FF_HIDDEN), const2, pipeline_mode=single),
                  pl.BlockSpec((FF_HIDDEN, d), const2, pipeline_mode=single)],
        out_specs=pl.BlockSpec((1, tm, d), row),
        out_shape=jax.ShapeDtypeStruct((bsz, L, d), F32),
        compiler_params=_params("arbitrary", "arbitrary"),
        name="ffn",
    )(x, ln_g.reshape(1, d), shift, scale, gate, w_in[:, :FF_HIDDEN], w_in[:, FF_HIDDEN:], w_out.astype(BF16))


def _rotary(t, cos, sin_signed):
    halves = [pltpu.roll(t[:, h * C_HEAD_DIM:(h + 1) * C_HEAD_DIM], C_HEAD_DIM // 2, 1) for h in range(C_HEADS)]
    return t * cos + jnp.concatenate(halves, axis=1) * sin_signed


def _proj_in1_body(x_ref, g_ref, sh_ref, sc_ref, w_ref, cos_ref, sin_ref,
                   cq_ref, ck_ref, cv_ref, cg_ref, dq_ref, dk_ref, dv_ref):
    h = _norm_mod(x_ref[0], g_ref[...], sh_ref[0], sc_ref[0]).astype(BF16)
    proj = lambda j: _dot(h, w_ref[:, j * WIDTH:(j + 1) * WIDTH])
    cos, sin = cos_ref[...], sin_ref[...]
    cq_ref[0] = _rotary(proj(0), cos, sin).astype(BF16)
    ck_ref[0] = (_rotary(proj(1), cos, sin) * (C_HEAD_DIM ** -0.5)).astype(BF16)
    cv_ref[0] = proj(2).astype(BF16)
    cg_ref[0] = proj(3).astype(BF16)
    dq_ref[0] = (proj(4) * SB_Q_SCALE).astype(BF16)
    dk_ref[0] = proj(5).astype(BF16)
    dv_ref[0] = proj(6).astype(BF16)


def rope_tables(L):
    half = C_HEAD_DIM // 2
    freqs = ROPE_BASE ** (-jnp.arange(half, dtype=F32) / half)
    ang = jnp.arange(L).astype(F32)[:, None] * freqs[None, :]
    cos, sin = jnp.cos(ang), jnp.sin(ang)
    cos_t = jnp.tile(jnp.concatenate([cos, cos], axis=1), (1, C_HEADS))
    sin_t = jnp.tile(jnp.concatenate([-sin, sin], axis=1), (1, C_HEADS))
    return cos_t, sin_t


def proj_in1(x, ln_g, shift, scale, w_in, tm=512):
    bsz, L, d = x.shape
    n = w_in.shape[1]
    cos_t, sin_t = rope_tables(L)
    row = lambda b, i: (b, i, 0)
    const2 = lambda b, i: (0, 0)
    per_b = lambda b, i: (b, 0, 0)
    pos = lambda b, i: (i, 0)
    out = jax.ShapeDtypeStruct((bsz, L, WIDTH), BF16)
    return pl.pallas_call(
        _proj_in1_body,
        grid=(bsz, L // tm),
        in_specs=[pl.BlockSpec((1, tm, d), row),
                  pl.BlockSpec((1, d), const2),
                  pl.BlockSpec((1, 1, d), per_b),
                  pl.BlockSpec((1, 1, d), per_b),
                  pl.BlockSpec((d, n), const2),
                  pl.BlockSpec((tm, WIDTH), pos),
                  pl.BlockSpec((tm, WIDTH), pos)],
        out_specs=[pl.BlockSpec((1, tm, WIDTH), row)] * 7,
        out_shape=[out] * 7,
        compiler_params=_params("arbitrary", "arbitrary"),
        name="proj_in1",
    )(x, ln_g.reshape(1, d), shift, scale, w_in.astype(BF16), cos_t, sin_t)


RET_ROWS = 256
RET_TILE = 512


def _retention_body(q_ref, k_ref, v_ref, g_ref, ng_ref, o_ref, state_ref, dmask_ref, qdec_ref, kdec_ref):
    R = RET_ROWS
    log_gammas = [math.log(1.0 - 2.0 ** (-5.0 - h)) for h in range(C_HEADS)]

    @pl.when((pl.program_id(0) == 0) & (pl.program_id(1) == 0))
    def _tables():
        ri = lax.broadcasted_iota(jnp.int32, (R, R), 0)
        ci = lax.broadcasted_iota(jnp.int32, (R, R), 1)
        rc, cc = ri // CHUNK, ci // CHUNK
        dist = jnp.where(rc == cc, jnp.abs(ri - ci), ri - ci).astype(F32)
        rown = lax.broadcasted_iota(jnp.int32, (R, C_HEAD_DIM), 0).astype(F32)
        for h, lg in enumerate(log_gammas):
            dmask_ref[h] = jnp.where(cc <= rc, jnp.exp(lg * dist), 0.0)
            qdec_ref[h] = jnp.exp(lg * (rown + 1.0))
            kdec_ref[h] = jnp.exp(lg * (R - 1.0 - rown))

    @pl.when(pl.program_id(1) == 0)
    def _reset():
        state_ref[...] = jnp.zeros(state_ref.shape, F32)

    for sub in range(RET_TILE // R):
        rows = slice(sub * R, (sub + 1) * R)
        for h, lg in enumerate(log_gammas):
            sl = slice(h * C_HEAD_DIM, (h + 1) * C_HEAD_DIM)
            q = q_ref[0, rows, sl]
            k = k_ref[0, rows, sl]
            v = v_ref[0, rows, sl]
            state = state_ref[h]
            scores = _dot_nt(q, k) * dmask_ref[h]
            o = (_dot(scores.astype(BF16), v)
                 + _dot((q.astype(F32) * qdec_ref[h]).astype(BF16), state.astype(BF16)))
            kd = (k.astype(F32) * kdec_ref[h]).T.astype(BF16)
            state_ref[h] = state * math.exp(lg * R) + _dot(kd, v)
            mu = jnp.mean(o, axis=-1, keepdims=True)
            cen = o - mu
            var = jnp.mean(cen * cen, axis=-1, keepdims=True)
            on = cen * lax.rsqrt(var + EPS) * ng_ref[:, sl]
            gt = g_ref[0, rows, sl].astype(F32)
            o_ref[0, rows, sl] = (gt * _sigmoid(gt) * on).astype(BF16)


def retention(q, k, v, g, norm_g):
    bsz, L, _ = q.shape
    blk = pl.BlockSpec((1, RET_TILE, C_WIDTH), lambda b, i: (b, i, 0))
    tab = pltpu.VMEM((C_HEADS, RET_ROWS, C_HEAD_DIM), F32)
    return pl.pallas_call(
        _retention_body,
        grid=(bsz, L // RET_TILE),
        in_specs=[blk, blk, blk, blk, pl.BlockSpec((1, C_WIDTH), lambda b, i: (0, 0))],
        out_specs=blk,
        out_shape=jax.ShapeDtypeStruct((bsz, L, C_WIDTH), BF16),
        scratch_shapes=[pltpu.VMEM((C_HEADS, C_HEAD_DIM, C_HEAD_DIM), F32),
                        pltpu.VMEM((C_HEADS, RET_ROWS, RET_ROWS), F32), tab, tab],
        compiler_params=_params("arbitrary", "arbitrary"),
        name="retention",
    )(q, k, v, g, norm_g.astype(F32).reshape(1, C_WIDTH))


SB_TILE = 256
SB_Q_SCALE = HEAD_DIM ** -0.5 * math.log2(math.e)


SB_GROUP = 2


def _stick_tiles(chains, kblk, vblk, tri, causal):
    zs = [_dot_nt(qm, kblk) for qm, _, _, _ in chains]
    sps = []
    for z in zs:
        neg_abs = pltpu.bitcast(pltpu.bitcast(z, jnp.uint32) | jnp.uint32(0x80000000), F32)
        sp = jnp.maximum(z, 0.0) + jnp.log2(1.0 + jnp.exp2(neg_abs))
        sps.append(sp if causal is None else jnp.where(causal, sp, 0.0))
    withins = [_dot(sp.astype(BF16), tri) for sp in sps]
    ws, laters = [], []
    for (_, _, lat_ref, r0), z, sp, within in zip(chains, zs, sps, withins):
        row_sum = jnp.sum(sp, axis=-1, keepdims=True)
        if causal is None:
            later = lat_ref[pl.ds(r0, SB_TILE), :]
            w = jnp.exp2(z - sp - within - jnp.concatenate([later, later], axis=1))
            laters.append(later + row_sum)
        else:
            w = jnp.where(causal, jnp.exp2(z - sp - within), 0.0)
            laters.append(jnp.broadcast_to(row_sum, (SB_TILE, LANES)))
        ws.append(w.astype(BF16))
    pvs = [_dot(w, vblk) for w in ws]
    for (_, acc_ref, lat_ref, r0), pv, later in zip(chains, pvs, laters):
        rows = pl.ds(r0, SB_TILE)
        acc_ref[rows, :] = pv if causal is not None else acc_ref[rows, :] + pv
        lat_ref[rows, :] = later


def _stick_body(q_ref, k_ref, v_ref, tri_ref, o_ref, acc0, acc1, lat0, lat1, *, nblk):
    T = SB_TILE
    lane = lax.broadcasted_iota(jnp.int32, (T, LANES), 1)
    low = lane < HEAD_DIM
    causal = lax.broadcasted_iota(jnp.int32, (T, T), 1) < lax.broadcasted_iota(jnp.int32, (T, T), 0)

    def chains_for(r0):
        qp = q_ref[0, pl.ds(r0, T), :]
        zero = jnp.zeros_like(qp)
        return [(jnp.where(low, qp, zero), acc0, lat0, r0), (jnp.where(low, zero, qp), acc1, lat1, r0)]

    def key_block(i, carry):
        kb = nblk - 1 - i
        k0 = pl.multiple_of(kb * T, T)
        kblk = k_ref[0, pl.ds(k0, T), :]
        vblk = v_ref[0, pl.ds(k0, T), :]
        tri = tri_ref[...]
        _stick_tiles(chains_for(k0), kblk, vblk, tri, causal)

        def below(t, c):
            chains = []
            for s in range(SB_GROUP):
                chains += chains_for(pl.multiple_of((kb + 1 + t * SB_GROUP + s) * T, T))
            _stick_tiles(chains, kblk, vblk, tri, None)
            return c

        n_groups = i // SB_GROUP
        lax.fori_loop(0, n_groups, below, 0)
        for s in range(SB_GROUP - 1):
            t_left = n_groups * SB_GROUP + s

            @pl.when(t_left < i)
            def _():
                _stick_tiles(chains_for(pl.multiple_of((kb + 1 + t_left) * T, T)), kblk, vblk, tri, None)
        return carry

    lax.fori_loop(0, nblk, key_block, 0)
    o_ref[0] = jnp.where(jnp.broadcast_to(lane[:1] < HEAD_DIM, acc0.shape), acc0[...], acc1[...]).astype(BF16)


def stick_breaking(q, k, v):
    bsz, L, _ = q.shape
    nblk = L // SB_TILE
    j = jnp.arange(SB_TILE)
    tri = (j[:, None] > j[None, :]).astype(BF16)
    blk = pl.BlockSpec((1, L, LANES), lambda b, h: (b, 0, h))
    vec = pltpu.VMEM((L, LANES), F32)
    return pl.pallas_call(
        functools.partial(_stick_body, nblk=nblk),
        grid=(bsz, D_HEADS // 2),
        in_specs=[blk, blk, blk, pl.BlockSpec((SB_TILE, SB_TILE), lambda b, h: (0, 0))],
        out_specs=blk,
        out_shape=jax.ShapeDtypeStruct((bsz, L, D_WIDTH), BF16),
        scratch_shapes=[vec, vec, vec, vec],
        compiler_params=_params("arbitrary", "arbitrary"),
        name="stick_breaking",
    )(q, k, v, tri)


def kernel(x, c, ada_w, ada_b, ln_mix_g, ln_ffn_g, ffn_w_in, ffn_w_out, ab_w_in, ab_w_out, a_q_gain, a_k_gain, a_rel_bias, s5_lambda_re, s5_lambda_im, s5_log_dt, s5_b_re, s5_b_im, s5_c_re, s5_c_im, s5_d, s5_w_glu, s5_b_glu, cd_w_in, cd_w_out, ret_norm_g):
    bsz, L, d = x.shape
    mod = ada_modulation(c, ada_w, ada_b)
    mods = [[mod[l, :, None, i * d:(i + 1) * d] for i in range(6)] for l in range(mod.shape[0])]

    sh1, sc1, g1, sh2, sc2, g2 = mods[0]
    q, k, v, u = proj_in0(x, ln_mix_g[0], sh1, sc1, ab_w_in[0], a_q_gain[0], a_k_gain[0])
    a_out = attn_a(q, k, v, a_rel_bias[0])
    ops = s5_operators(s5_lambda_re[0], s5_lambda_im[0], s5_log_dt[0], s5_b_re[0], s5_b_im[0],
                       s5_c_re[0], s5_c_im[0], s5_d[0])
    y_gelu = s5_mixer_gelu(u, ops)
    x = proj_out0(x, a_out, y_gelu, s5_w_glu[0], s5_b_glu[0], ab_w_out[0], g1)
    x = ffn(x, ln_ffn_g[0], sh2, sc2, g2, ffn_w_in[0], ffn_w_out[0])

    sh1, sc1, g1, sh2, sc2, g2 = mods[1]
    cq, ck, cv, cg, dq, dk, dv = proj_in1(x, ln_mix_g[1], sh1, sc1, cd_w_in[0])
    c_out = retention(cq, ck, cv, cg, ret_norm_g[0])
    d_out = stick_breaking(dq, dk, dv)
    x = proj_out1(x, c_out, d_out, cd_w_out[0], g1)
    x = ffn(x, ln_ffn_g[1], sh2, sc2, g2, ffn_w_in[1], ffn_w_out[1])
    return x
```

```python
import functools
import math

import jax
import jax.numpy as jnp
from jax import lax
from jax.experimental import pallas as pl
from jax.experimental.pallas import tpu as pltpu

F32 = jnp.float32
BF16 = jnp.bfloat16

D_MODEL = 1024
CHUNK = 64
HEAD_DIM = 64
EPS = 1e-6
A_WIDTH = 512
A_HEADS = 8
A_LEFT_CHUNKS = 8
A_BAND = (A_LEFT_CHUNKS + 1) * CHUNK
A_PAD = A_LEFT_CHUNKS * CHUNK
A_MAX_REL = 128
B_WIDTH = 512
B_GROUP = 16
B_GROUPS = 32
B_STATE = 64
C_WIDTH = 512
C_HEADS = 4
C_HEAD_DIM = 128
ROPE_BASE = 10000.0
D_WIDTH = 512
D_HEADS = 8
FF_HIDDEN = 2816
WIDTH = 512
LANES = 128
NEG_BIG = -1e30
A_Q_SCALE = HEAD_DIM ** -0.5 * math.log2(math.e)

VMEM_LIMIT = 56 * 1024 * 1024


def _params(*sem):
    return pltpu.CompilerParams(dimension_semantics=sem, vmem_limit_bytes=VMEM_LIMIT)


def _sigmoid(x):
    return 1.0 / (1.0 + jnp.exp(-x))


def _gelu_tanh(x):
    return 0.5 * x * (1.0 + jnp.tanh(math.sqrt(2.0 / math.pi) * (x + 0.044715 * (x * x * x))))


def _norm_mod(x, g, shift, scale):
    ms = jnp.mean(x * x, axis=-1, keepdims=True)
    y = x * lax.rsqrt(ms + EPS) * g
    return y * (1.0 + scale) + shift


def _dot(a, b):
    return jnp.dot(a, b, preferred_element_type=F32)


def _dot_nt(a, b):
    return lax.dot_general(a, b, (((1,), (1,)), ((), ())), preferred_element_type=F32)


def _ada_body(c_ref, w_ref, b_ref, o_ref):
    c = c_ref[...]
    cond = c * _sigmoid(c)
    o_ref[0] = jnp.dot(cond, w_ref[0], preferred_element_type=F32,
                       precision=lax.Precision.HIGHEST) + b_ref[0]


def ada_modulation(c, ada_w, ada_b):
    depth, d, n = ada_w.shape
    bsz = c.shape[0]
    tn = 1536
    return pl.pallas_call(
        _ada_body,
        grid=(depth, n // tn),
        in_specs=[pl.BlockSpec((bsz, d), lambda l, j: (0, 0)),
                  pl.BlockSpec((1, d, tn), lambda l, j: (l, 0, j)),
                  pl.BlockSpec((1, 1, tn), lambda l, j: (l, 0, j))],
        out_specs=pl.BlockSpec((1, bsz, tn), lambda l, j: (l, 0, j)),
        out_shape=jax.ShapeDtypeStruct((depth, bsz, n), F32),
        compiler_params=_params("arbitrary", "arbitrary"),
        name="ada_modulation",
    )(c, ada_w, ada_b.reshape(depth, 1, n))


def _proj_in0_body(x_ref, g_ref, sh_ref, sc_ref, w_ref, qg_ref, kg_ref, bd_ref, kz_ref, vz_ref,
                   q_ref, k_ref, v_ref, u_ref):
    del kz_ref, vz_ref
    h = _norm_mod(x_ref[0], g_ref[...], sh_ref[0], sc_ref[0]).astype(BF16)

    def head_norm(t, gain):
        ms = _dot((t * t).astype(BF16), bd_ref[...])
        return t * lax.rsqrt(ms + EPS) * gain

    q_ref[0] = head_norm(_dot(h, w_ref[:, 0:WIDTH]), qg_ref[...]).astype(BF16)
    k_ref[0] = head_norm(_dot(h, w_ref[:, WIDTH:2 * WIDTH]), kg_ref[...]).astype(BF16)
    v_ref[0] = _dot(h, w_ref[:, 2 * WIDTH:3 * WIDTH]).astype(BF16)
    u_ref[0] = _dot(h, w_ref[:, 3 * WIDTH:4 * WIDTH]).astype(BF16)


def proj_in0(x, ln_g, shift, scale, w_in, q_gain, k_gain, tm=512):
    bsz, L, d = x.shape
    n = w_in.shape[1]
    assert A_PAD % tm == 0
    head_of = jnp.arange(WIDTH) // HEAD_DIM
    bd = (head_of[:, None] == head_of[None, :]).astype(BF16) * (1.0 / HEAD_DIM)
    qg = (jnp.tile(q_gain.astype(F32), A_HEADS) * A_Q_SCALE).reshape(1, WIDTH)
    kg = jnp.tile(k_gain.astype(F32), A_HEADS).reshape(1, WIDTH)
    row = lambda b, i: (b, i, 0)
    padded_row = lambda b, i: (b, i + A_PAD // tm, 0)
    const2 = lambda b, i: (0, 0)
    per_b = lambda b, i: (b, 0, 0)
    out = jax.ShapeDtypeStruct((bsz, L, WIDTH), BF16)
    padded = jax.ShapeDtypeStruct((bsz, L + A_PAD, WIDTH), BF16)
    zeros = jnp.zeros(padded.shape, BF16)
    hbm = pl.BlockSpec(memory_space=pl.ANY)
    return pl.pallas_call(
        _proj_in0_body,
        grid=(bsz, L // tm),
        in_specs=[pl.BlockSpec((1, tm, d), row),
                  pl.BlockSpec((1, d), const2),
                  pl.BlockSpec((1, 1, d), per_b),
                  pl.BlockSpec((1, 1, d), per_b),
                  pl.BlockSpec((d, n), const2),
                  pl.BlockSpec((1, WIDTH), const2),
                  pl.BlockSpec((1, WIDTH), const2),
                  pl.BlockSpec((WIDTH, WIDTH), const2),
                  hbm, hbm],
        out_specs=[pl.BlockSpec((1, tm, WIDTH), row),
                   pl.BlockSpec((1, tm, WIDTH), padded_row),
                   pl.BlockSpec((1, tm, WIDTH), padded_row),
                   pl.BlockSpec((1, tm, WIDTH), row)],
        out_shape=[out, padded, padded, out],
        input_output_aliases={8: 1, 9: 2},
        compiler_params=_params("arbitrary", "arbitrary"),
        name="proj_in0",
    )(x, ln_g.reshape(1, d), shift, scale, w_in.astype(BF16), qg, kg, bd, zeros, zeros)


def _attn_a_body(q_ref, k_ref, v_ref, bias_ref, o_ref, *, tq):
    j = pl.program_id(1)
    win = tq + A_PAD
    start = pl.multiple_of(j * tq, tq)
    col = lax.broadcasted_iota(jnp.int32, (tq, win), 1)
    valid = col >= A_PAD - j * tq
    lane = lax.broadcasted_iota(jnp.int32, (tq, LANES), 1)
    low = lane < HEAD_DIM
    slices = [slice(hp * LANES, (hp + 1) * LANES) for hp in range(A_HEADS // 2)]
    vws = [v_ref[0, pl.ds(start, win), sl] for sl in slices]
    scores = []
    for hp, sl in enumerate(slices):
        qp = q_ref[0, :, sl]
        kw = k_ref[0, pl.ds(start, win), sl]
        zero = jnp.zeros_like(qp)
        for e in range(2):
            qm = jnp.where(low, qp, zero) if e == 0 else jnp.where(low, zero, qp)
            scores.append(jnp.where(valid, _dot_nt(qm, kw) + bias_ref[2 * hp + e], NEG_BIG))
    probs, inv_sums = [], []
    for s in scores:
        p = jnp.exp2(s - jnp.max(s, axis=-1, keepdims=True))
        inv_sums.append(1.0 / jnp.sum(p, axis=-1, keepdims=True))
        probs.append(p.astype(BF16))
    outs = [_dot(p, vws[h // 2]) * inv for h, (p, inv) in enumerate(zip(probs, inv_sums))]
    for hp, sl in enumerate(slices):
        o_ref[0, :, sl] = jnp.where(low, outs[2 * hp], outs[2 * hp + 1]).astype(BF16)


def attn_a_bias_table(rel_bias, tq):
    win = tq + A_PAD
    period = tq + win
    m = jnp.arange(period)
    diff = jnp.where(m < win, m, m - period)
    idx = jnp.clip(A_PAD - diff, -A_MAX_REL, A_MAX_REL) + A_MAX_REL
    x = rel_bias.astype(F32)[:, idx] * math.log2(math.e)
    tab = jnp.tile(x, (1, tq + 1))[:, :tq * (period - 1)].reshape(-1, tq, period - 1)[:, :, :win]
    qi = jnp.arange(tq)
    band = jnp.arange(win)[None, :] - (qi[:, None] // CHUNK) * CHUNK
    inside = (band >= 0) & (band < A_BAND)
    return jnp.where(inside[None], tab, NEG_BIG)


def attn_a(q, kp, vp, rel_bias, tq=256):
    bsz, L, _ = q.shape
    win = tq + A_PAD
    bias = attn_a_bias_table(rel_bias, tq)
    per_b = lambda b, i: (b, 0, 0)
    return pl.pallas_call(
        functools.partial(_attn_a_body, tq=tq),
        grid=(bsz, L // tq),
        in_specs=[pl.BlockSpec((1, tq, WIDTH), lambda b, i: (b, i, 0)),
                  pl.BlockSpec((1, L + A_PAD, WIDTH), per_b),
                  pl.BlockSpec((1, L + A_PAD, WIDTH), per_b),
                  pl.BlockSpec((A_HEADS, tq, win), lambda b, i: (0, 0, 0), pipeline_mode=pl.Buffered(1))],
        out_specs=pl.BlockSpec((1, tq, WIDTH), lambda b, i: (b, i, 0)),
        out_shape=jax.ShapeDtypeStruct((bsz, L, WIDTH), BF16),
        compiler_params=_params("arbitrary", "arbitrary"),
        name="attn_a",
    )(q, kp, vp, bias)


S5_LAGS = 72


def _s5_ops_body(lrc_ref, lic_ref, lrr_ref, lir_ref, ldt_ref, btr_ref, bti_ref, ctr_ref, cti_ref, dpad_ref,
                 e1_ref, e2_ref, m_ref, v_ref, w_ref, a_ref):
    hp = lax.Precision.HIGHEST
    kdim = CHUNK * B_GROUP
    dt = jnp.exp(ldt_ref[0])
    n_l = lax.broadcasted_iota(jnp.int32, (B_STATE, LANES), 1).astype(F32)
    mag = jnp.exp(lrc_ref[0] * dt * n_l)
    ang = lic_ref[0] * dt * n_l
    pw_re, pw_im = mag * jnp.cos(ang), mag * jnp.sin(ang)
    lam_re, lam_im = lrr_ref[0], lir_ref[0]
    n_s = lax.broadcasted_iota(jnp.int32, (LANES, B_STATE), 0).astype(F32)
    mag_t = jnp.exp(lam_re * dt * n_s)
    ang_t = lam_im * dt * n_s
    pwt_re, pwt_im = mag_t * jnp.cos(ang_t), mag_t * jnp.sin(ang_t)
    ab_re, ab_im = pwt_re[1:2, :], pwt_im[1:2, :]
    den = lam_re * lam_re + lam_im * lam_im
    f_re = ((ab_re - 1.0) * lam_re + ab_im * lam_im) / den
    f_im = (ab_im * lam_re - (ab_re - 1.0) * lam_im) / den
    bt_re, bt_im = btr_ref[0], bti_ref[0]
    bb_re = f_re * bt_re - f_im * bt_im
    bb_im = f_re * bt_im + f_im * bt_re
    rep = lambda t, e_ref: jnp.dot(t, e_ref[...], preferred_element_type=F32, precision=hp)
    a_re, a_im = rep(pw_re, e1_ref), rep(pw_im, e1_ref)
    c_re, c_im = rep(ctr_ref[0], e2_ref), rep(cti_ref[0], e2_ref)
    ca_re = c_re * a_re - c_im * a_im
    ca_im = c_re * a_im + c_im * a_re
    kern = (jnp.dot(bb_re, ca_re, preferred_element_type=F32, precision=hp)
            - jnp.dot(bb_im, ca_im, preferred_element_type=F32, precision=hp))[:, :kdim]
    row = lax.broadcasted_iota(jnp.int32, (B_GROUP, kdim), 0)
    lane = lax.broadcasted_iota(jnp.int32, (B_GROUP, kdim), 1)
    kern = kern + jnp.where(lane == row, dpad_ref[0], 0.0)
    for s in range(CHUNK):
        shifted = kern if s == 0 else pltpu.roll(kern, s * B_GROUP, 1)
        m_ref[0, s * B_GROUP:(s + 1) * B_GROUP, :] = jnp.where(lane >= s * B_GROUP, shifted, 0.0).astype(BF16)
        r_re = pwt_re[CHUNK - 1 - s:CHUNK - s, :]
        r_im = pwt_im[CHUNK - 1 - s:CHUNK - s, :]
        v_ref[0, s * B_GROUP:(s + 1) * B_GROUP, 0:B_STATE] = (r_re * bb_re - r_im * bb_im).astype(BF16)
        v_ref[0, s * B_GROUP:(s + 1) * B_GROUP, B_STATE:2 * B_STATE] = (r_re * bb_im + r_im * bb_re).astype(BF16)
    width = S5_LAGS * B_GROUP
    w_ref[0, 0:B_STATE, :] = pltpu.roll(ca_re, width - B_GROUP, 1)[:, :kdim].astype(BF16)
    w_ref[0, B_STATE:2 * B_STATE, :] = (-pltpu.roll(ca_im, width - B_GROUP, 1)[:, :kdim]).astype(BF16)
    a64_re, a64_im = pwt_re[CHUNK:CHUNK + 1, :], pwt_im[CHUNK:CHUNK + 1, :]
    a_ref[0, 0:1, 0:B_STATE] = a64_re
    a_ref[0, 0:1, B_STATE:2 * B_STATE] = a64_re
    a_ref[0, 1:2, 0:B_STATE] = -a64_im
    a_ref[0, 1:2, B_STATE:2 * B_STATE] = a64_im


def s5_operators(lam_re, lam_im, log_dt, b_re, b_im, c_re, c_im, d_skip):
    g, p = lam_re.shape
    kdim = CHUNK * B_GROUP
    width = S5_LAGS * B_GROUP
    f = lambda t: t.astype(F32)
    col = jnp.arange(width)
    e1 = (jnp.arange(LANES)[:, None] == col[None, :] // B_GROUP).astype(F32)
    e2 = (jnp.arange(B_GROUP)[:, None] == col[None, :] % B_GROUP).astype(F32)
    dpad = jnp.pad(f(d_skip), ((0, 0), (0, kdim - B_GROUP))).reshape(g, 1, kdim)
    per_g = lambda i: (i, 0, 0)
    const2 = lambda i: (0, 0)
    blk = lambda *shape: pl.BlockSpec((1,) + shape, per_g)
    return pl.pallas_call(
        _s5_ops_body,
        grid=(g,),
        in_specs=[blk(p, 1), blk(p, 1), blk(1, p), blk(1, p), blk(1, 1),
                  blk(B_GROUP, p), blk(B_GROUP, p), blk(p, B_GROUP), blk(p, B_GROUP), blk(1, kdim),
                  pl.BlockSpec((LANES, width), const2), pl.BlockSpec((B_GROUP, width), const2)],
        out_specs=[blk(kdim, kdim), blk(kdim, 2 * p), blk(2 * p, kdim), blk(2, 2 * p)],
        out_shape=[jax.ShapeDtypeStruct((g, kdim, kdim), BF16),
                   jax.ShapeDtypeStruct((g, kdim, 2 * p), BF16),
                   jax.ShapeDtypeStruct((g, 2 * p, kdim), BF16),
                   jax.ShapeDtypeStruct((g, 2, 2 * p), F32)],
        compiler_params=_params("arbitrary"),
        name="s5_operators",
    )(f(lam_re).reshape(g, p, 1), f(lam_im).reshape(g, p, 1), f(lam_re).reshape(g, 1, p), f(lam_im).reshape(g, 1, p),
      f(log_dt).reshape(g, 1, 1), f(b_re).transpose(0, 2, 1), f(b_im).transpose(0, 2, 1),
      f(c_re).transpose(0, 2, 1), f(c_im).transpose(0, 2, 1), dpad, e1, e2)


def _s5_body(u_ref, m_ref, v_ref, w_ref, a_ref, y_ref, s_ref, xs_ref, *, nc, bsz):
    u = u_ref[0]
    s_ref[...] = _dot(u, v_ref[0])
    a1 = a_ref[0, 0:1, :]
    a2 = a_ref[0, 1:2, :]

    def step(c, x):
        r = pl.multiple_of(c * bsz, bsz)
        xs_ref[pl.ds(r, bsz), :] = x
        return x * a1 + pltpu.roll(x, B_STATE, 1) * a2 + s_ref[pl.ds(r, bsz), :]

    lax.fori_loop(0, nc, step, jnp.zeros((bsz, 2 * B_STATE), F32))
    y = _dot(u, m_ref[0]) + _dot(xs_ref[...].astype(BF16), w_ref[0])
    y_ref[0] = _gelu_tanh(y).astype(BF16)


def s5_mixer_gelu(u, ops):
    m, v, w, a = ops
    bsz, L, _ = u.shape
    nc = L // CHUNK
    n = nc * bsz
    kdim = CHUNK * B_GROUP
    ug = u.reshape(bsz, nc, CHUNK, B_GROUPS, B_GROUP).transpose(3, 1, 0, 2, 4).reshape(B_GROUPS, n, kdim)
    per_g = lambda g: (g, 0, 0)
    yg = pl.pallas_call(
        functools.partial(_s5_body, nc=nc, bsz=bsz),
        grid=(B_GROUPS,),
        in_specs=[pl.BlockSpec((1, n, kdim), per_g),
                  pl.BlockSpec((1, kdim, kdim), per_g),
                  pl.BlockSpec((1, kdim, 2 * B_STATE), per_g),
                  pl.BlockSpec((1, 2 * B_STATE, kdim), per_g),
                  pl.BlockSpec((1, 2, 2 * B_STATE), per_g)],
        out_specs=pl.BlockSpec((1, n, kdim), per_g),
        out_shape=jax.ShapeDtypeStruct((B_GROUPS, n, kdim), BF16),
        scratch_shapes=[pltpu.VMEM((n, 2 * B_STATE), F32), pltpu.VMEM((n, 2 * B_STATE), F32)],
        compiler_params=_params("arbitrary"),
        name="s5_mixer",
    )(ug, m, v, w, a)
    return yg.reshape(B_GROUPS, nc, bsz, CHUNK, B_GROUP).transpose(2, 1, 3, 0, 4).reshape(bsz, L, B_WIDTH)


FFN_CHUNK = 256


def _mix_ffn_body(*refs, glu):
    if glu:
        (x_ref, s1_ref, s2_ref, wglu_ref, bglu_ref, wo_ref, g1_ref, lng_ref, sh_ref, sc_ref, g2_ref,
         wg_ref, wu_ref, wd_ref, o_ref) = refs
        y = s2_ref[0]
        gate = _sigmoid(_dot(y, wglu_ref[...]) + bglu_ref[...])
        s2 = (y.astype(F32) * gate).astype(BF16)
    else:
        (x_ref, s1_ref, s2_ref, wo_ref, g1_ref, lng_ref, sh_ref, sc_ref, g2_ref,
         wg_ref, wu_ref, wd_ref, o_ref) = refs
        s2 = s2_ref[0]
    mix = _dot(s1_ref[0], wo_ref[0:WIDTH, :]) + _dot(s2, wo_ref[WIDTH:2 * WIDTH, :])
    x1 = x_ref[0] + g1_ref[0] * mix
    h = _norm_mod(x1, lng_ref[...], sh_ref[0], sc_ref[0]).astype(BF16)
    acc = jnp.zeros(x1.shape, F32)
    for f in range(FF_HIDDEN // FFN_CHUNK):
        sl = slice(f * FFN_CHUNK, (f + 1) * FFN_CHUNK)
        gt = _dot(h, wg_ref[:, sl])
        up = _dot(h, wu_ref[:, sl])
        act = (gt * _sigmoid(gt) * up).astype(BF16)
        acc = acc + _dot(act, wd_ref[sl, :])
    o_ref[0] = x1 + g2_ref[0] * acc


def mix_ffn(x, s1, s2, w_out, g1, ln_g, shift, scale, g2, w_in, w_down, glu=None, tm=512):
    bsz, L, d = x.shape
    row = lambda b, i: (b, i, 0)
    const2 = lambda b, i: (0, 0)
    per_b = lambda b, i: (b, 0, 0)
    single = pl.Buffered(1)
    weight = lambda shape: pl.BlockSpec(shape, const2, pipeline_mode=single)
    vec = pl.BlockSpec((1, 1, d), per_b)
    w_in = w_in.astype(BF16)
    in_specs = [pl.BlockSpec((1, tm, d), row), pl.BlockSpec((1, tm, WIDTH), row), pl.BlockSpec((1, tm, WIDTH), row)]
    args = [x, s1, s2]
    if glu is not None:
        in_specs += [weight((WIDTH, WIDTH)), pl.BlockSpec((1, WIDTH), const2)]
        args += [glu[0].astype(BF16), glu[1].astype(F32).reshape(1, WIDTH)]
    in_specs += [weight((2 * WIDTH, d)), vec, pl.BlockSpec((1, d), const2), vec, vec, vec,
                 weight((d, FF_HIDDEN)), weight((d, FF_HIDDEN)), weight((FF_HIDDEN, d))]
    args += [w_out.astype(BF16), g1, ln_g.reshape(1, d), shift, scale, g2,
             w_in[:, :FF_HIDDEN], w_in[:, FF_HIDDEN:], w_down.astype(BF16)]
    return pl.pallas_call(
        functools.partial(_mix_ffn_body, glu=glu is not None),
        grid=(bsz, L // tm),
        in_specs=in_specs,
        out_specs=pl.BlockSpec((1, tm, d), row),
        out_shape=jax.ShapeDtypeStruct((bsz, L, d), F32),
        compiler_params=_params("arbitrary", "arbitrary"),
        name="mix_ffn",
    )(*args)


def _rotary(t, cos, sin_signed):
    halves = [pltpu.roll(t[:, h * C_HEAD_DIM:(h + 1) * C_HEAD_DIM], C_HEAD_DIM // 2, 1) for h in range(C_HEADS)]
    return t * cos + jnp.concatenate(halves, axis=1) * sin_signed


def _proj_in1_body(x_ref, g_ref, sh_ref, sc_ref, w_ref, cos_ref, sin_ref,
                   cq_ref, ck_ref, cv_ref, cg_ref, dq_ref, dk_ref, dv_ref):
    h = _norm_mod(x_ref[0], g_ref[...], sh_ref[0], sc_ref[0]).astype(BF16)
    proj = lambda j: _dot(h, w_ref[:, j * WIDTH:(j + 1) * WIDTH])
    cos, sin = cos_ref[...], sin_ref[...]
    cq_ref[0] = _rotary(proj(0), cos, sin).astype(BF16)
    ck_ref[0] = (_rotary(proj(1), cos, sin) * (C_HEAD_DIM ** -0.5)).astype(BF16)
    cv_ref[0] = proj(2).astype(BF16)
    cg_ref[0] = proj(3).astype(BF16)
    dq_ref[0] = (proj(4) * SB_Q_SCALE).astype(BF16)
    dk_ref[0] = proj(5).astype(BF16)
    dv_ref[0] = proj(6).astype(BF16)


def rope_tables(L):
    half = C_HEAD_DIM // 2
    freqs = ROPE_BASE ** (-jnp.arange(half, dtype=F32) / half)
    ang = jnp.arange(L).astype(F32)[:, None] * freqs[None, :]
    cos, sin = jnp.cos(ang), jnp.sin(ang)
    cos_t = jnp.tile(jnp.concatenate([cos, cos], axis=1), (1, C_HEADS))
    sin_t = jnp.tile(jnp.concatenate([-sin, sin], axis=1), (1, C_HEADS))
    return cos_t, sin_t


def proj_in1(x, ln_g, shift, scale, w_in, tm=512):
    bsz, L, d = x.shape
    n = w_in.shape[1]
    cos_t, sin_t = rope_tables(L)
    row = lambda b, i: (b, i, 0)
    const2 = lambda b, i: (0, 0)
    per_b = lambda b, i: (b, 0, 0)
    pos = lambda b, i: (i, 0)
    out = jax.ShapeDtypeStruct((bsz, L, WIDTH), BF16)
    return pl.pallas_call(
        _proj_in1_body,
        grid=(bsz, L // tm),
        in_specs=[pl.BlockSpec((1, tm, d), row),
                  pl.BlockSpec((1, d), const2),
                  pl.BlockSpec((1, 1, d), per_b),
                  pl.BlockSpec((1, 1, d), per_b),
                  pl.BlockSpec((d, n), const2),
                  pl.BlockSpec((tm, WIDTH), pos),
                  pl.BlockSpec((tm, WIDTH), pos)],
        out_specs=[pl.BlockSpec((1, tm, WIDTH), row)] * 7,
        out_shape=[out] * 7,
        compiler_params=_params("arbitrary", "arbitrary"),
        name="proj_in1",
    )(x, ln_g.reshape(1, d), shift, scale, w_in.astype(BF16), cos_t, sin_t)


RET_ROWS = 256
RET_TILE = 512


def _retention_body(q_ref, k_ref, v_ref, g_ref, ng_ref, o_ref, state_ref, dmask_ref, qdec_ref, kdec_ref):
    R = RET_ROWS
    log_gammas = [math.log(1.0 - 2.0 ** (-5.0 - h)) for h in range(C_HEADS)]

    @pl.when((pl.program_id(0) == 0) & (pl.program_id(1) == 0))
    def _tables():
        ri = lax.broadcasted_iota(jnp.int32, (R, R), 0)
        ci = lax.broadcasted_iota(jnp.int32, (R, R), 1)
        rc, cc = ri // CHUNK, ci // CHUNK
        dist = jnp.where(rc == cc, jnp.abs(ri - ci), ri - ci).astype(F32)
        rown = lax.broadcasted_iota(jnp.int32, (R, C_HEAD_DIM), 0).astype(F32)
        for h, lg in enumerate(log_gammas):
            dmask_ref[h] = jnp.where(cc <= rc, jnp.exp(lg * dist), 0.0)
            qdec_ref[h] = jnp.exp(lg * (rown + 1.0))
            kdec_ref[h] = jnp.exp(lg * (R - 1.0 - rown))

    @pl.when(pl.program_id(1) == 0)
    def _reset():
        state_ref[...] = jnp.zeros(state_ref.shape, F32)

    for sub in range(RET_TILE // R):
        rows = slice(sub * R, (sub + 1) * R)
        for h, lg in enumerate(log_gammas):
            sl = slice(h * C_HEAD_DIM, (h + 1) * C_HEAD_DIM)
            q = q_ref[0, rows, sl]
            k = k_ref[0, rows, sl]
            v = v_ref[0, rows, sl]
            state = state_ref[h]
            scores = _dot_nt(q, k) * dmask_ref[h]
            o = (_dot(scores.astype(BF16), v)
                 + _dot((q.astype(F32) * qdec_ref[h]).astype(BF16), state.astype(BF16)))
            kd = (k.astype(F32) * kdec_ref[h]).T.astype(BF16)
            state_ref[h] = state * math.exp(lg * R) + _dot(kd, v)
            mu = jnp.mean(o, axis=-1, keepdims=True)
            cen = o - mu
            var = jnp.mean(cen * cen, axis=-1, keepdims=True)
            on = cen * lax.rsqrt(var + EPS) * ng_ref[:, sl]
            gt = g_ref[0, rows, sl].astype(F32)
            o_ref[0, rows, sl] = (gt * _sigmoid(gt) * on).astype(BF16)


def retention(q, k, v, g, norm_g):
    bsz, L, _ = q.shape
    blk = pl.BlockSpec((1, RET_TILE, C_WIDTH), lambda b, i: (b, i, 0))
    tab = pltpu.VMEM((C_HEADS, RET_ROWS, C_HEAD_DIM), F32)
    return pl.pallas_call(
        _retention_body,
        grid=(bsz, L // RET_TILE),
        in_specs=[blk, blk, blk, blk, pl.BlockSpec((1, C_WIDTH), lambda b, i: (0, 0))],
        out_specs=blk,
        out_shape=jax.ShapeDtypeStruct((bsz, L, C_WIDTH), BF16),
        scratch_shapes=[pltpu.VMEM((C_HEADS, C_HEAD_DIM, C_HEAD_DIM), F32),
                        pltpu.VMEM((C_HEADS, RET_ROWS, RET_ROWS), F32), tab, tab],
        compiler_params=_params("arbitrary", "arbitrary"),
        name="retention",
    )(q, k, v, g, norm_g.astype(F32).reshape(1, C_WIDTH))


SB_TILE = 256
SB_Q_SCALE = HEAD_DIM ** -0.5 * math.log2(math.e)
SB_GROUP = 4
SB_CHAINS = 2 * SB_GROUP


def _sb_scores(chains, kblk):
    return [_dot_nt(qm, kblk) for qm, _, _, _ in chains]


def _sb_softplus(zs, causal):
    sps = []
    for z in zs:
        neg_abs = pltpu.bitcast(pltpu.bitcast(z, jnp.uint32) | jnp.uint32(0x80000000), F32)
        sp = jnp.maximum(z, 0.0) + jnp.log2(1.0 + jnp.exp2(neg_abs))
        sps.append(sp if causal is None else jnp.where(causal, sp, 0.0))
    return sps


def _sb_suffix(sps, tri):
    return [_dot(sp.astype(BF16), tri) for sp in sps]


def _sb_log_weights(chains, zs, sps, withins, causal):
    out = []
    for (_, _, lat_ref, r0), z, sp, within in zip(chains, zs, sps, withins):
        rows = pl.ds(r0, SB_TILE)
        row_sum = jnp.sum(sp, axis=-1, keepdims=True)
        if causal is None:
            later = lat_ref[rows, :]
            out.append(z - sp - within - jnp.concatenate([later, later], axis=1))
            lat_ref[rows, :] = later + row_sum
        else:
            out.append(jnp.where(causal, z - sp - within, NEG_BIG))
            lat_ref[rows, :] = jnp.broadcast_to(row_sum, (SB_TILE, LANES))
    return out


def _sb_first_half(chains, kblk, tri, causal):
    zs = _sb_scores(chains, kblk)
    sps = _sb_softplus(zs, causal)
    return _sb_log_weights(chains, zs, sps, _sb_suffix(sps, tri), causal)


def _sb_weights(logws):
    return [jnp.exp2(t).astype(BF16) for t in logws]


def _sb_accumulate(chains, pvs, first):
    for (_, acc_ref, _, r0), pv in zip(chains, pvs):
        rows = pl.ds(r0, SB_TILE)
        acc_ref[rows, :] = pv if first else acc_ref[rows, :] + pv


def _sb_second_half(chains, logws, vblk, first):
    _sb_accumulate(chains, [_dot(w, vblk) for w in _sb_weights(logws)], first)


def _stick_body(q_ref, k_ref, v_ref, tri_ref, o_ref, acc0, acc1, lat0, lat1, logw_ref, *, nblk):
    T = SB_TILE
    lane = lax.broadcasted_iota(jnp.int32, (T, LANES), 1)
    low = lane < HEAD_DIM
    causal = lax.broadcasted_iota(jnp.int32, (T, T), 1) < lax.broadcasted_iota(jnp.int32, (T, T), 0)

    def chains_for(r0):
        qp = q_ref[0, pl.ds(r0, T), :]
        zero = jnp.zeros_like(qp)
        return [(jnp.where(low, qp, zero), acc0, lat0, r0), (jnp.where(low, zero, qp), acc1, lat1, r0)]

    def key_block(i, carry):
        kb = nblk - 1 - i
        k0 = pl.multiple_of(kb * T, T)
        kblk = k_ref[0, pl.ds(k0, T), :]
        vblk = v_ref[0, pl.ds(k0, T), :]
        tri = tri_ref[...]

        def tiles(first_tile, count):
            chains = []
            for s in range(count):
                chains += chains_for(pl.multiple_of((kb + 1 + first_tile + s) * T, T))
            return chains

        def stash(slot, logws):
            for c, t in enumerate(logws):
                logw_ref[slot, c] = t

        def fetch(slot):
            return [logw_ref[slot, c] for c in range(SB_CHAINS)]

        diag = chains_for(k0)
        _sb_second_half(diag, _sb_first_half(diag, kblk, tri, causal), vblk, True)

        n_groups = i // SB_GROUP

        @pl.when(n_groups > 0)
        def _pipelined():
            stash(0, _sb_first_half(tiles(0, SB_GROUP), kblk, tri, None))

            def step(t, c):
                cur, nxt = tiles(t * SB_GROUP, SB_GROUP), tiles((t + 1) * SB_GROUP, SB_GROUP)
                zs = _sb_scores(nxt, kblk)
                ws = _sb_weights(fetch(t % 2))
                sps = _sb_softplus(zs, None)
                pvs = [_dot(w, vblk) for w in ws]
                withins = _sb_suffix(sps, tri)
                _sb_accumulate(cur, pvs, False)
                stash((t + 1) % 2, _sb_log_weights(nxt, zs, sps, withins, None))
                return c

            lax.fori_loop(0, n_groups - 1, step, 0)
            last = n_groups - 1
            _sb_second_half(tiles(last * SB_GROUP, SB_GROUP), fetch(last % 2), vblk, False)

        done = n_groups * SB_GROUP
        left = i - done
        run = SB_GROUP // 2
        while run >= 1:
            @pl.when((left // run) % 2 == 1)
            def _(run=run):
                chains = tiles(done + (left // (2 * run)) * (2 * run), run)
                _sb_second_half(chains, _sb_first_half(chains, kblk, tri, None), vblk, False)
            run //= 2
        return carry

    lax.fori_loop(0, nblk, key_block, 0)
    o_ref[0] = jnp.where(jnp.broadcast_to(lane[:1] < HEAD_DIM, acc0.shape), acc0[...], acc1[...]).astype(BF16)


def stick_breaking(q, k, v):
    bsz, L, _ = q.shape
    nblk = L // SB_TILE
    j = jnp.arange(SB_TILE)
    tri = (j[:, None] > j[None, :]).astype(BF16)
    blk = pl.BlockSpec((1, L, LANES), lambda b, h: (b, 0, h))
    vec = pltpu.VMEM((L, LANES), F32)
    return pl.pallas_call(
        functools.partial(_stick_body, nblk=nblk),
        grid=(bsz, D_HEADS // 2),
        in_specs=[blk, blk, blk, pl.BlockSpec((SB_TILE, SB_TILE), lambda b, h: (0, 0))],
        out_specs=blk,
        out_shape=jax.ShapeDtypeStruct((bsz, L, D_WIDTH), BF16),
        scratch_shapes=[vec, vec, vec, vec, pltpu.VMEM((2, SB_CHAINS, SB_TILE, SB_TILE), F32)],
        compiler_params=_params("arbitrary", "arbitrary"),
        name="stick_breaking",
    )(q, k, v, tri)


def kernel(x, c, ada_w, ada_b, ln_mix_g, ln_ffn_g, ffn_w_in, ffn_w_out, ab_w_in, ab_w_out, a_q_gain, a_k_gain, a_rel_bias, s5_lambda_re, s5_lambda_im, s5_log_dt, s5_b_re, s5_b_im, s5_c_re, s5_c_im, s5_d, s5_w_glu, s5_b_glu, cd_w_in, cd_w_out, ret_norm_g):
    bsz, L, d = x.shape
    mod = ada_modulation(c, ada_w, ada_b)
    mods = [[mod[l, :, None, i * d:(i + 1) * d] for i in range(6)] for l in range(mod.shape[0])]

    sh1, sc1, g1, sh2, sc2, g2 = mods[0]
    q, k, v, u = proj_in0(x, ln_mix_g[0], sh1, sc1, ab_w_in[0], a_q_gain[0], a_k_gain[0])
    a_out = attn_a(q, k, v, a_rel_bias[0])
    ops = s5_operators(s5_lambda_re[0], s5_lambda_im[0], s5_log_dt[0], s5_b_re[0], s5_b_im[0],
                       s5_c_re[0], s5_c_im[0], s5_d[0])
    y_gelu = s5_mixer_gelu(u, ops)
    x = mix_ffn(x, a_out, y_gelu, ab_w_out[0], g1, ln_ffn_g[0], sh2, sc2, g2, ffn_w_in[0], ffn_w_out[0],
                glu=(s5_w_glu[0], s5_b_glu[0]))

    sh1, sc1, g1, sh2, sc2, g2 = mods[1]
    cq, ck, cv, cg, dq, dk, dv = proj_in1(x, ln_mix_g[1], sh1, sc1, cd_w_in[0])
    c_out = retention(cq, ck, cv, cg, ret_norm_g[0])
    d_out = stick_breaking(dq, dk, dv)
    x = mix_ffn(x, c_out, d_out, cd_w_out[0], g1, ln_ffn_g[1], sh2, sc2, g2, ffn_w_in[1], ffn_w_out[1])
    return x
```

```python
import functools
import math

import jax
import jax.numpy as jnp
from jax import lax
from jax.experimental import pallas as pl
from jax.experimental.pallas import tpu as pltpu

F32 = jnp.float32
BF16 = jnp.bfloat16

D_MODEL = 1024
CHUNK = 64
HEAD_DIM = 64
EPS = 1e-6
A_WIDTH = 512
A_HEADS = 8
A_LEFT_CHUNKS = 8
A_BAND = (A_LEFT_CHUNKS + 1) * CHUNK
A_PAD = A_LEFT_CHUNKS * CHUNK
A_MAX_REL = 128
B_WIDTH = 512
B_GROUP = 16
B_GROUPS = 32
B_STATE = 64
C_WIDTH = 512
C_HEADS = 4
C_HEAD_DIM = 128
ROPE_BASE = 10000.0
D_WIDTH = 512
D_HEADS = 8
FF_HIDDEN = 2816
WIDTH = 512
LANES = 128
NEG_BIG = -1e30
A_Q_SCALE = HEAD_DIM ** -0.5 * math.log2(math.e)

VMEM_LIMIT = 56 * 1024 * 1024


def _params(*sem):
    return pltpu.CompilerParams(dimension_semantics=sem, vmem_limit_bytes=VMEM_LIMIT)


def _sigmoid(x):
    return 1.0 / (1.0 + jnp.exp(-x))


def _gelu_tanh(x):
    return 0.5 * x * (1.0 + jnp.tanh(math.sqrt(2.0 / math.pi) * (x + 0.044715 * (x * x * x))))


def _norm_mod(x, g, shift, scale):
    ms = jnp.mean(x * x, axis=-1, keepdims=True)
    y = x * lax.rsqrt(ms + EPS) * g
    return y * (1.0 + scale) + shift


def _dot(a, b):
    return jnp.dot(a, b, preferred_element_type=F32)


def _dot_nt(a, b):
    return lax.dot_general(a, b, (((1,), (1,)), ((), ())), preferred_element_type=F32)


def _ada_body(c_ref, w_ref, b_ref, o_ref):
    c = c_ref[...]
    cond = c * _sigmoid(c)
    o_ref[0] = jnp.dot(cond, w_ref[0], preferred_element_type=F32,
                       precision=lax.Precision.HIGHEST) + b_ref[0]


def ada_modulation(c, ada_w, ada_b):
    depth, d, n = ada_w.shape
    bsz = c.shape[0]
    tn = 1536
    return pl.pallas_call(
        _ada_body,
        grid=(depth, n // tn),
        in_specs=[pl.BlockSpec((bsz, d), lambda l, j: (0, 0)),
                  pl.BlockSpec((1, d, tn), lambda l, j: (l, 0, j)),
                  pl.BlockSpec((1, 1, tn), lambda l, j: (l, 0, j))],
        out_specs=pl.BlockSpec((1, bsz, tn), lambda l, j: (l, 0, j)),
        out_shape=jax.ShapeDtypeStruct((depth, bsz, n), F32),
        compiler_params=_params("arbitrary", "arbitrary"),
        name="ada_modulation",
    )(c, ada_w, ada_b.reshape(depth, 1, n))


def _proj_in0_body(x_ref, g_ref, sh_ref, sc_ref, w_ref, qg_ref, kg_ref, bd_ref, kz_ref, vz_ref,
                   q_ref, k_ref, v_ref, u_ref):
    del kz_ref, vz_ref
    h = _norm_mod(x_ref[0], g_ref[...], sh_ref[0], sc_ref[0]).astype(BF16)

    def head_norm(t, gain):
        ms = _dot((t * t).astype(BF16), bd_ref[...])
        return t * lax.rsqrt(ms + EPS) * gain

    q_ref[0] = head_norm(_dot(h, w_ref[:, 0:WIDTH]), qg_ref[...]).astype(BF16)
    k_ref[0] = head_norm(_dot(h, w_ref[:, WIDTH:2 * WIDTH]), kg_ref[...]).astype(BF16)
    v_ref[0] = _dot(h, w_ref[:, 2 * WIDTH:3 * WIDTH]).astype(BF16)
    u_ref[0] = _dot(h, w_ref[:, 3 * WIDTH:4 * WIDTH]).astype(BF16)


def proj_in0(x, ln_g, shift, scale, w_in, q_gain, k_gain, tm=512):
    bsz, L, d = x.shape
    n = w_in.shape[1]
    assert A_PAD % tm == 0
    head_of = jnp.arange(WIDTH) // HEAD_DIM
    bd = (head_of[:, None] == head_of[None, :]).astype(BF16) * (1.0 / HEAD_DIM)
    qg = (jnp.tile(q_gain.astype(F32), A_HEADS) * A_Q_SCALE).reshape(1, WIDTH)
    kg = jnp.tile(k_gain.astype(F32), A_HEADS).reshape(1, WIDTH)
    row = lambda b, i: (b, i, 0)
    padded_row = lambda b, i: (b, i + A_PAD // tm, 0)
    const2 = lambda b, i: (0, 0)
    per_b = lambda b, i: (b, 0, 0)
    out = jax.ShapeDtypeStruct((bsz, L, WIDTH), BF16)
    padded = jax.ShapeDtypeStruct((bsz, L + A_PAD, WIDTH), BF16)
    zeros = jnp.zeros(padded.shape, BF16)
    hbm = pl.BlockSpec(memory_space=pl.ANY)
    return pl.pallas_call(
        _proj_in0_body,
        grid=(bsz, L // tm),
        in_specs=[pl.BlockSpec((1, tm, d), row),
                  pl.BlockSpec((1, d), const2),
                  pl.BlockSpec((1, 1, d), per_b),
                  pl.BlockSpec((1, 1, d), per_b),
                  pl.BlockSpec((d, n), const2),
                  pl.BlockSpec((1, WIDTH), const2),
                  pl.BlockSpec((1, WIDTH), const2),
                  pl.BlockSpec((WIDTH, WIDTH), const2),
                  hbm, hbm],
        out_specs=[pl.BlockSpec((1, tm, WIDTH), row),
                   pl.BlockSpec((1, tm, WIDTH), padded_row),
                   pl.BlockSpec((1, tm, WIDTH), padded_row),
                   pl.BlockSpec((1, tm, WIDTH), row)],
        out_shape=[out, padded, padded, out],
        input_output_aliases={8: 1, 9: 2},
        compiler_params=_params("arbitrary", "arbitrary"),
        name="proj_in0",
    )(x, ln_g.reshape(1, d), shift, scale, w_in.astype(BF16), qg, kg, bd, zeros, zeros)


def _attn_a_body(q_ref, k_ref, v_ref, bias_ref, o_ref, *, tq):
    j = pl.program_id(1)
    win = tq + A_PAD
    start = pl.multiple_of(j * tq, tq)
    col = lax.broadcasted_iota(jnp.int32, (tq, win), 1)
    valid = col >= A_PAD - j * tq
    lane = lax.broadcasted_iota(jnp.int32, (tq, LANES), 1)
    low = lane < HEAD_DIM
    slices = [slice(hp * LANES, (hp + 1) * LANES) for hp in range(A_HEADS // 2)]
    vws = [v_ref[0, pl.ds(start, win), sl] for sl in slices]
    scores = []
    for hp, sl in enumerate(slices):
        qp = q_ref[0, :, sl]
        kw = k_ref[0, pl.ds(start, win), sl]
        zero = jnp.zeros_like(qp)
        for e in range(2):
            qm = jnp.where(low, qp, zero) if e == 0 else jnp.where(low, zero, qp)
            scores.append(jnp.where(valid, _dot_nt(qm, kw) + bias_ref[2 * hp + e], NEG_BIG))
    probs, inv_sums = [], []
    for s in scores:
        p = jnp.exp2(s - jnp.max(s, axis=-1, keepdims=True))
        inv_sums.append(1.0 / jnp.sum(p, axis=-1, keepdims=True))
        probs.append(p.astype(BF16))
    outs = [_dot(p, vws[h // 2]) * inv for h, (p, inv) in enumerate(zip(probs, inv_sums))]
    for hp, sl in enumerate(slices):
        o_ref[0, :, sl] = jnp.where(low, outs[2 * hp], outs[2 * hp + 1]).astype(BF16)


def attn_a_bias_table(rel_bias, tq):
    win = tq + A_PAD
    period = tq + win
    m = jnp.arange(period)
    diff = jnp.where(m < win, m, m - period)
    idx = jnp.clip(A_PAD - diff, -A_MAX_REL, A_MAX_REL) + A_MAX_REL
    x = rel_bias.astype(F32)[:, idx] * math.log2(math.e)
    tab = jnp.tile(x, (1, tq + 1))[:, :tq * (period - 1)].reshape(-1, tq, period - 1)[:, :, :win]
    qi = jnp.arange(tq)
    band = jnp.arange(win)[None, :] - (qi[:, None] // CHUNK) * CHUNK
    inside = (band >= 0) & (band < A_BAND)
    return jnp.where(inside[None], tab, NEG_BIG)


def attn_a(q, kp, vp, rel_bias, tq=256):
    bsz, L, _ = q.shape
    win = tq + A_PAD
    bias = attn_a_bias_table(rel_bias, tq)
    per_b = lambda b, i: (b, 0, 0)
    return pl.pallas_call(
        functools.partial(_attn_a_body, tq=tq),
        grid=(bsz, L // tq),
        in_specs=[pl.BlockSpec((1, tq, WIDTH), lambda b, i: (b, i, 0)),
                  pl.BlockSpec((1, L + A_PAD, WIDTH), per_b),
                  pl.BlockSpec((1, L + A_PAD, WIDTH), per_b),
                  pl.BlockSpec((A_HEADS, tq, win), lambda b, i: (0, 0, 0), pipeline_mode=pl.Buffered(1))],
        out_specs=pl.BlockSpec((1, tq, WIDTH), lambda b, i: (b, i, 0)),
        out_shape=jax.ShapeDtypeStruct((bsz, L, WIDTH), BF16),
        compiler_params=_params("arbitrary", "arbitrary"),
        name="attn_a",
    )(q, kp, vp, bias)


S5_LAGS = 72


def _s5_ops_body(lrc_ref, lic_ref, lrr_ref, lir_ref, ldt_ref, btr_ref, bti_ref, ctr_ref, cti_ref, dpad_ref,
                 e1_ref, e2_ref, m_ref, v_ref, w_ref, a_ref):
    hp = lax.Precision.HIGHEST
    kdim = CHUNK * B_GROUP
    dt = jnp.exp(ldt_ref[0])
    n_l = lax.broadcasted_iota(jnp.int32, (B_STATE, LANES), 1).astype(F32)
    mag = jnp.exp(lrc_ref[0] * dt * n_l)
    ang = lic_ref[0] * dt * n_l
    pw_re, pw_im = mag * jnp.cos(ang), mag * jnp.sin(ang)
    lam_re, lam_im = lrr_ref[0], lir_ref[0]
    n_s = lax.broadcasted_iota(jnp.int32, (LANES, B_STATE), 0).astype(F32)
    mag_t = jnp.exp(lam_re * dt * n_s)
    ang_t = lam_im * dt * n_s
    pwt_re, pwt_im = mag_t * jnp.cos(ang_t), mag_t * jnp.sin(ang_t)
    ab_re, ab_im = pwt_re[1:2, :], pwt_im[1:2, :]
    den = lam_re * lam_re + lam_im * lam_im
    f_re = ((ab_re - 1.0) * lam_re + ab_im * lam_im) / den
    f_im = (ab_im * lam_re - (ab_re - 1.0) * lam_im) / den
    bt_re, bt_im = btr_ref[0], bti_ref[0]
    bb_re = f_re * bt_re - f_im * bt_im
    bb_im = f_re * bt_im + f_im * bt_re
    rep = lambda t, e_ref: jnp.dot(t, e_ref[...], preferred_element_type=F32, precision=hp)
    a_re, a_im = rep(pw_re, e1_ref), rep(pw_im, e1_ref)
    c_re, c_im = rep(ctr_ref[0], e2_ref), rep(cti_ref[0], e2_ref)
    ca_re = c_re * a_re - c_im * a_im
    ca_im = c_re * a_im + c_im * a_re
    kern = (jnp.dot(bb_re, ca_re, preferred_element_type=F32, precision=hp)
            - jnp.dot(bb_im, ca_im, preferred_element_type=F32, precision=hp))[:, :kdim]
    row = lax.broadcasted_iota(jnp.int32, (B_GROUP, kdim), 0)
    lane = lax.broadcasted_iota(jnp.int32, (B_GROUP, kdim), 1)
    kern = kern + jnp.where(lane == row, dpad_ref[0], 0.0)
    for s in range(CHUNK):
        shifted = kern if s == 0 else pltpu.roll(kern, s * B_GROUP, 1)
        m_ref[0, s * B_GROUP:(s + 1) * B_GROUP, :] = jnp.where(lane >= s * B_GROUP, shifted, 0.0).astype(BF16)
        r_re = pwt_re[CHUNK - 1 - s:CHUNK - s, :]
        r_im = pwt_im[CHUNK - 1 - s:CHUNK - s, :]
        v_ref[0, s * B_GROUP:(s + 1) * B_GROUP, 0:B_STATE] = (r_re * bb_re - r_im * bb_im).astype(BF16)
        v_ref[0, s * B_GROUP:(s + 1) * B_GROUP, B_STATE:2 * B_STATE] = (r_re * bb_im + r_im * bb_re).astype(BF16)
    width = S5_LAGS * B_GROUP
    w_ref[0, 0:B_STATE, :] = pltpu.roll(ca_re, width - B_GROUP, 1)[:, :kdim].astype(BF16)
    w_ref[0, B_STATE:2 * B_STATE, :] = (-pltpu.roll(ca_im, width - B_GROUP, 1)[:, :kdim]).astype(BF16)
    a64_re, a64_im = pwt_re[CHUNK:CHUNK + 1, :], pwt_im[CHUNK:CHUNK + 1, :]
    a_ref[0, 0:1, 0:B_STATE] = a64_re
    a_ref[0, 0:1, B_STATE:2 * B_STATE] = a64_re
    a_ref[0, 1:2, 0:B_STATE] = -a64_im
    a_ref[0, 1:2, B_STATE:2 * B_STATE] = a64_im


def s5_operators(lam_re, lam_im, log_dt, b_re, b_im, c_re, c_im, d_skip):
    g, p = lam_re.shape
    kdim = CHUNK * B_GROUP
    width = S5_LAGS * B_GROUP
    f = lambda t: t.astype(F32)
    col = jnp.arange(width)
    e1 = (jnp.arange(LANES)[:, None] == col[None, :] // B_GROUP).astype(F32)
    e2 = (jnp.arange(B_GROUP)[:, None] == col[None, :] % B_GROUP).astype(F32)
    dpad = jnp.pad(f(d_skip), ((0, 0), (0, kdim - B_GROUP))).reshape(g, 1, kdim)
    per_g = lambda i: (i, 0, 0)
    const2 = lambda i: (0, 0)
    blk = lambda *shape: pl.BlockSpec((1,) + shape, per_g)
    return pl.pallas_call(
        _s5_ops_body,
        grid=(g,),
        in_specs=[blk(p, 1), blk(p, 1), blk(1, p), blk(1, p), blk(1, 1),
                  blk(B_GROUP, p), blk(B_GROUP, p), blk(p, B_GROUP), blk(p, B_GROUP), blk(1, kdim),
                  pl.BlockSpec((LANES, width), const2), pl.BlockSpec((B_GROUP, width), const2)],
        out_specs=[blk(kdim, kdim), blk(kdim, 2 * p), blk(2 * p, kdim), blk(2, 2 * p)],
        out_shape=[jax.ShapeDtypeStruct((g, kdim, kdim), BF16),
                   jax.ShapeDtypeStruct((g, kdim, 2 * p), BF16),
                   jax.ShapeDtypeStruct((g, 2 * p, kdim), BF16),
                   jax.ShapeDtypeStruct((g, 2, 2 * p), F32)],
        compiler_params=_params("arbitrary"),
        name="s5_operators",
    )(f(lam_re).reshape(g, p, 1), f(lam_im).reshape(g, p, 1), f(lam_re).reshape(g, 1, p), f(lam_im).reshape(g, 1, p),
      f(log_dt).reshape(g, 1, 1), f(b_re).transpose(0, 2, 1), f(b_im).transpose(0, 2, 1),
      f(c_re).transpose(0, 2, 1), f(c_im).transpose(0, 2, 1), dpad, e1, e2)


def _s5_body(u_ref, m_ref, v_ref, w_ref, a_ref, y_ref, xre_ref, xim_ref, *, nc, bsz):
    u = u_ref[0]
    s = _dot(u, v_ref[0])
    s_re, s_im = s[:, 0:B_STATE], s[:, B_STATE:2 * B_STATE]
    a_re = a_ref[0, 0:1, 0:B_STATE]
    a_im = a_ref[0, 1:2, B_STATE:2 * B_STATE]
    x_re = x_im = jnp.zeros((bsz, B_STATE), F32)
    for c in range(nc):
        rows = slice(c * bsz, (c + 1) * bsz)
        xre_ref[rows, :] = x_re
        xim_ref[rows, :] = x_im
        x_re, x_im = (x_re * a_re - x_im * a_im + s_re[rows, :],
                      x_re * a_im + x_im * a_re + s_im[rows, :])
    y = (_dot(u, m_ref[0]) + _dot(xre_ref[...].astype(BF16), w_ref[0, 0:B_STATE, :])
         + _dot(xim_ref[...].astype(BF16), w_ref[0, B_STATE:2 * B_STATE, :]))
    y_ref[0] = _gelu_tanh(y).astype(BF16)


def s5_mixer_gelu(u, ops):
    m, v, w, a = ops
    bsz, L, _ = u.shape
    nc = L // CHUNK
    n = nc * bsz
    kdim = CHUNK * B_GROUP
    ug = u.reshape(bsz, nc, CHUNK, B_GROUPS, B_GROUP).transpose(3, 1, 0, 2, 4).reshape(B_GROUPS, n, kdim)
    per_g = lambda g: (g, 0, 0)
    yg = pl.pallas_call(
        functools.partial(_s5_body, nc=nc, bsz=bsz),
        grid=(B_GROUPS,),
        in_specs=[pl.BlockSpec((1, n, kdim), per_g),
                  pl.BlockSpec((1, kdim, kdim), per_g),
                  pl.BlockSpec((1, kdim, 2 * B_STATE), per_g),
                  pl.BlockSpec((1, 2 * B_STATE, kdim), per_g),
                  pl.BlockSpec((1, 2, 2 * B_STATE), per_g)],
        out_specs=pl.BlockSpec((1, n, kdim), per_g),
        out_shape=jax.ShapeDtypeStruct((B_GROUPS, n, kdim), BF16),
        scratch_shapes=[pltpu.VMEM((n, B_STATE), F32), pltpu.VMEM((n, B_STATE), F32)],
        compiler_params=_params("arbitrary"),
        name="s5_mixer",
    )(ug, m, v, w, a)
    return yg.reshape(B_GROUPS, nc, bsz, CHUNK, B_GROUP).transpose(2, 1, 3, 0, 4).reshape(bsz, L, B_WIDTH)


FFN_CHUNK = 256


def _mix_ffn_body(*refs, glu):
    if glu:
        (x_ref, s1_ref, s2_ref, wglu_ref, bglu_ref, wo_ref, g1_ref, lng_ref, sh_ref, sc_ref, g2_ref,
         wg_ref, wu_ref, wd_ref, o_ref) = refs
        y = s2_ref[0]
        gate = _sigmoid(_dot(y, wglu_ref[...]) + bglu_ref[...])
        s2 = (y.astype(F32) * gate).astype(BF16)
    else:
        (x_ref, s1_ref, s2_ref, wo_ref, g1_ref, lng_ref, sh_ref, sc_ref, g2_ref,
         wg_ref, wu_ref, wd_ref, o_ref) = refs
        s2 = s2_ref[0]
    mix = _dot(s1_ref[0], wo_ref[0:WIDTH, :]) + _dot(s2, wo_ref[WIDTH:2 * WIDTH, :])
    x1 = x_ref[0] + g1_ref[0] * mix
    h = _norm_mod(x1, lng_ref[...], sh_ref[0], sc_ref[0]).astype(BF16)
    acc = jnp.zeros(x1.shape, F32)
    for f in range(FF_HIDDEN // FFN_CHUNK):
        sl = slice(f * FFN_CHUNK, (f + 1) * FFN_CHUNK)
        gt = _dot(h, wg_ref[:, sl])
        up = _dot(h, wu_ref[:, sl])
        act = (gt * _sigmoid(gt) * up).astype(BF16)
        acc = acc + _dot(act, wd_ref[sl, :])
    o_ref[0] = x1 + g2_ref[0] * acc


def mix_ffn(x, s1, s2, w_out, g1, ln_g, shift, scale, g2, w_in, w_down, glu=None, tm=512):
    bsz, L, d = x.shape
    row = lambda b, i: (b, i, 0)
    const2 = lambda b, i: (0, 0)
    per_b = lambda b, i: (b, 0, 0)
    single = pl.Buffered(1)
    weight = lambda shape: pl.BlockSpec(shape, const2, pipeline_mode=single)
    vec = pl.BlockSpec((1, 1, d), per_b)
    w_in = w_in.astype(BF16)
    in_specs = [pl.BlockSpec((1, tm, d), row), pl.BlockSpec((1, tm, WIDTH), row), pl.BlockSpec((1, tm, WIDTH), row)]
    args = [x, s1, s2]
    if glu is not None:
        in_specs += [weight((WIDTH, WIDTH)), pl.BlockSpec((1, WIDTH), const2)]
        args += [glu[0].astype(BF16), glu[1].astype(F32).reshape(1, WIDTH)]
    in_specs += [weight((2 * WIDTH, d)), vec, pl.BlockSpec((1, d), const2), vec, vec, vec,
                 weight((d, FF_HIDDEN)), weight((d, FF_HIDDEN)), weight((FF_HIDDEN, d))]
    args += [w_out.astype(BF16), g1, ln_g.reshape(1, d), shift, scale, g2,
             w_in[:, :FF_HIDDEN], w_in[:, FF_HIDDEN:], w_down.astype(BF16)]
    return pl.pallas_call(
        functools.partial(_mix_ffn_body, glu=glu is not None),
        grid=(bsz, L // tm),
        in_specs=in_specs,
        out_specs=pl.BlockSpec((1, tm, d), row),
        out_shape=jax.ShapeDtypeStruct((bsz, L, d), F32),
        compiler_params=_params("arbitrary", "arbitrary"),
        name="mix_ffn",
    )(*args)


def _rotary(t, cos, sin_signed):
    halves = [pltpu.roll(t[:, h * C_HEAD_DIM:(h + 1) * C_HEAD_DIM], C_HEAD_DIM // 2, 1) for h in range(C_HEADS)]
    return t * cos + jnp.concatenate(halves, axis=1) * sin_signed


def _proj_in1_body(x_ref, g_ref, sh_ref, sc_ref, w_ref, cos_ref, sin_ref,
                   cq_ref, ck_ref, cv_ref, cg_ref, dq_ref, dk_ref, dv_ref):
    h = _norm_mod(x_ref[0], g_ref[...], sh_ref[0], sc_ref[0]).astype(BF16)
    proj = lambda j: _dot(h, w_ref[:, j * WIDTH:(j + 1) * WIDTH])
    cos, sin = cos_ref[...], sin_ref[...]
    cq_ref[0] = _rotary(proj(0), cos, sin).astype(BF16)
    ck_ref[0] = (_rotary(proj(1), cos, sin) * (C_HEAD_DIM ** -0.5)).astype(BF16)
    cv_ref[0] = proj(2).astype(BF16)
    cg_ref[0] = proj(3).astype(BF16)
    dq_ref[0] = (proj(4) * SB_Q_SCALE).astype(BF16)
    dk_ref[0] = proj(5).astype(BF16)
    dv_ref[0] = proj(6).astype(BF16)


def rope_tables(L):
    half = C_HEAD_DIM // 2
    freqs = ROPE_BASE ** (-jnp.arange(half, dtype=F32) / half)
    ang = jnp.arange(L).astype(F32)[:, None] * freqs[None, :]
    cos, sin = jnp.cos(ang), jnp.sin(ang)
    cos_t = jnp.tile(jnp.concatenate([cos, cos], axis=1), (1, C_HEADS))
    sin_t = jnp.tile(jnp.concatenate([-sin, sin], axis=1), (1, C_HEADS))
    return cos_t, sin_t


def proj_in1(x, ln_g, shift, scale, w_in, tm=512):
    bsz, L, d = x.shape
    n = w_in.shape[1]
    cos_t, sin_t = rope_tables(L)
    row = lambda b, i: (b, i, 0)
    const2 = lambda b, i: (0, 0)
    per_b = lambda b, i: (b, 0, 0)
    pos = lambda b, i: (i, 0)
    out = jax.ShapeDtypeStruct((bsz, L, WIDTH), BF16)
    return pl.pallas_call(
        _proj_in1_body,
        grid=(bsz, L // tm),
        in_specs=[pl.BlockSpec((1, tm, d), row),
                  pl.BlockSpec((1, d), const2),
                  pl.BlockSpec((1, 1, d), per_b),
                  pl.BlockSpec((1, 1, d), per_b),
                  pl.BlockSpec((d, n), const2),
                  pl.BlockSpec((tm, WIDTH), pos),
                  pl.BlockSpec((tm, WIDTH), pos)],
        out_specs=[pl.BlockSpec((1, tm, WIDTH), row)] * 7,
        out_shape=[out] * 7,
        compiler_params=_params("arbitrary", "arbitrary"),
        name="proj_in1",
    )(x, ln_g.reshape(1, d), shift, scale, w_in.astype(BF16), cos_t, sin_t)


RET_ROWS = 256
RET_TILE = 512


def _retention_body(q_ref, k_ref, v_ref, g_ref, ng_ref, o_ref, state_ref, dmask_ref, qdec_ref, kdec_ref):
    R = RET_ROWS
    log_gammas = [math.log(1.0 - 2.0 ** (-5.0 - h)) for h in range(C_HEADS)]

    @pl.when((pl.program_id(0) == 0) & (pl.program_id(1) == 0))
    def _tables():
        ri = lax.broadcasted_iota(jnp.int32, (R, R), 0)
        ci = lax.broadcasted_iota(jnp.int32, (R, R), 1)
        rc, cc = ri // CHUNK, ci // CHUNK
        dist = jnp.where(rc == cc, jnp.abs(ri - ci), ri - ci).astype(F32)
        rown = lax.broadcasted_iota(jnp.int32, (R, C_HEAD_DIM), 0).astype(F32)
        for h, lg in enumerate(log_gammas):
            dmask_ref[h] = jnp.where(cc <= rc, jnp.exp(lg * dist), 0.0)
            qdec_ref[h] = jnp.exp(lg * (rown + 1.0))
            kdec_ref[h] = jnp.exp(lg * (R - 1.0 - rown))

    @pl.when(pl.program_id(1) == 0)
    def _reset():
        state_ref[...] = jnp.zeros(state_ref.shape, F32)

    for sub in range(RET_TILE // R):
        rows = slice(sub * R, (sub + 1) * R)
        for h, lg in enumerate(log_gammas):
            sl = slice(h * C_HEAD_DIM, (h + 1) * C_HEAD_DIM)
            q = q_ref[0, rows, sl]
            k = k_ref[0, rows, sl]
            v = v_ref[0, rows, sl]
            state = state_ref[h]
            scores = _dot_nt(q, k) * dmask_ref[h]
            o = (_dot(scores.astype(BF16), v)
                 + _dot((q.astype(F32) * qdec_ref[h]).astype(BF16), state.astype(BF16)))
            kd = (k.astype(F32) * kdec_ref[h]).T.astype(BF16)
            state_ref[h] = state * math.exp(lg * R) + _dot(kd, v)
            mu = jnp.mean(o, axis=-1, keepdims=True)
            cen = o - mu
            var = jnp.mean(cen * cen, axis=-1, keepdims=True)
            on = cen * lax.rsqrt(var + EPS) * ng_ref[:, sl]
            gt = g_ref[0, rows, sl].astype(F32)
            o_ref[0, rows, sl] = (gt * _sigmoid(gt) * on).astype(BF16)


def retention(q, k, v, g, norm_g):
    bsz, L, _ = q.shape
    blk = pl.BlockSpec((1, RET_TILE, C_WIDTH), lambda b, i: (b, i, 0))
    tab = pltpu.VMEM((C_HEADS, RET_ROWS, C_HEAD_DIM), F32)
    return pl.pallas_call(
        _retention_body,
        grid=(bsz, L // RET_TILE),
        in_specs=[blk, blk, blk, blk, pl.BlockSpec((1, C_WIDTH), lambda b, i: (0, 0))],
        out_specs=blk,
        out_shape=jax.ShapeDtypeStruct((bsz, L, C_WIDTH), BF16),
        scratch_shapes=[pltpu.VMEM((C_HEADS, C_HEAD_DIM, C_HEAD_DIM), F32),
                        pltpu.VMEM((C_HEADS, RET_ROWS, RET_ROWS), F32), tab, tab],
        compiler_params=_params("arbitrary", "arbitrary"),
        name="retention",
    )(q, k, v, g, norm_g.astype(F32).reshape(1, C_WIDTH))


SB_TILE = 256
SB_Q_SCALE = HEAD_DIM ** -0.5 * math.log2(math.e)
SB_GROUP = 4
SB_CHAINS = 2 * SB_GROUP


def _sb_scores(chains, kblk):
    return [_dot_nt(qm, kblk) for qm, _, _, _ in chains]


def _sb_softplus(zs, causal):
    sps = []
    for z in zs:
        neg_abs = pltpu.bitcast(pltpu.bitcast(z, jnp.uint32) | jnp.uint32(0x80000000), F32)
        sp = jnp.maximum(z, 0.0) + jnp.log2(1.0 + jnp.exp2(neg_abs))
        sps.append(sp if causal is None else jnp.where(causal, sp, 0.0))
    return sps


def _sb_suffix(sps, tri):
    return [_dot(sp.astype(BF16), tri) for sp in sps]


def _sb_log_weights(chains, zs, sps, withins, causal):
    out = []
    for (_, _, lat_ref, r0), z, sp, within in zip(chains, zs, sps, withins):
        rows = pl.ds(r0, SB_TILE)
        row_sum = jnp.sum(sp, axis=-1, keepdims=True)
        if causal is None:
            later = lat_ref[rows, :]
            out.append(z - sp - within - jnp.concatenate([later, later], axis=1))
            lat_ref[rows, :] = later + row_sum
        else:
            out.append(jnp.where(causal, z - sp - within, NEG_BIG))
            lat_ref[rows, :] = jnp.broadcast_to(row_sum, (SB_TILE, LANES))
    return out


def _sb_first_half(chains, kblk, tri, causal):
    zs = _sb_scores(chains, kblk)
    sps = _sb_softplus(zs, causal)
    return _sb_log_weights(chains, zs, sps, _sb_suffix(sps, tri), causal)


def _sb_weights(logws):
    return [jnp.exp2(t).astype(BF16) for t in logws]


def _sb_accumulate(chains, pvs, first):
    for (_, acc_ref, _, r0), pv in zip(chains, pvs):
        rows = pl.ds(r0, SB_TILE)
        acc_ref[rows, :] = pv if first else acc_ref[rows, :] + pv


def _sb_second_half(chains, logws, vblk, first):
    _sb_accumulate(chains, [_dot(w, vblk) for w in _sb_weights(logws)], first)


def _stick_body(q_ref, k_ref, v_ref, tri_ref, o_ref, acc0, acc1, lat0, lat1, logw_ref, qm_ref, *, nblk):
    T, G = SB_TILE, SB_GROUP
    accs, lats = (acc0, acc1), (lat0, lat1)
    lane = lax.broadcasted_iota(jnp.int32, (T, LANES), 1)
    low = lane < HEAD_DIM
    causal = lax.broadcasted_iota(jnp.int32, (T, T), 1) < lax.broadcasted_iota(jnp.int32, (T, T), 0)
    tri = tri_ref[...]

    def group_body(g, carry):
        base = g * G
        rows = [pl.ds(pl.multiple_of((base + a) * T, T), T) for a in range(G)]
        for a in range(G):
            qp = q_ref[0, rows[a], :]
            zero = jnp.zeros_like(qp)
            qm_ref[a, 0] = jnp.where(low, qp, zero)
            qm_ref[a, 1] = jnp.where(low, zero, qp)

        kblks = [k_ref[0, rows[b], :] for b in range(G)]
        vblks = [v_ref[0, rows[b], :] for b in range(G)]
        tiles = [(a, b, e) for b in reversed(range(G)) for a in range(b, G) for e in range(2)]
        zs = {c: _dot_nt(qm_ref[c[0], c[2]], kblks[c[1]]) for c in tiles}
        sps = {c: _sb_softplus([zs[c]], causal if c[0] == c[1] else None)[0] for c in tiles}
        withins = {c: _dot(sps[c].astype(BF16), tri) for c in tiles}
        later, logws = {}, {}
        for c in tiles:
            a, b, e = c
            row_sum = jnp.sum(sps[c], axis=-1, keepdims=True)
            if a == b:
                logws[c] = jnp.where(causal, zs[c] - sps[c] - withins[c], NEG_BIG)
                later[a, e] = jnp.broadcast_to(row_sum, (T, LANES))
            else:
                lt = later[a, e]
                logws[c] = zs[c] - sps[c] - withins[c] - jnp.concatenate([lt, lt], axis=1)
                later[a, e] = lt + row_sum
        ws = {c: jnp.exp2(logws[c]).astype(BF16) for c in tiles}
        pvs = {c: _dot(ws[c], vblks[c[1]]) for c in tiles}
        for a in range(G):
            for e in range(2):
                accs[e][rows[a], :] = functools.reduce(lambda x, y: x + y, [pvs[a, b, e] for b in range(a, -1, -1)])
                lats[e][rows[a], :] = later[a, e]

        def chains():
            return [(qm_ref[a, e], accs[e], lats[e], pl.multiple_of((base + a) * T, T))
                    for a in range(G) for e in range(2)]

        def kv(kb):
            r = pl.ds(pl.multiple_of(kb * T, T), T)
            return k_ref[0, r, :], v_ref[0, r, :]

        def stash(slot, vals):
            for c, t in enumerate(vals):
                logw_ref[slot, c] = t

        def fetch(slot):
            return [logw_ref[slot, c] for c in range(SB_CHAINS)]

        @pl.when(base > 0)
        def _earlier():
            stash(0, _sb_first_half(chains(), kv(base - 1)[0], tri, None))

            def step(t, c):
                ch = chains()
                _, v_cur = kv(base - 1 - t)
                k_next, _ = kv(base - 2 - t)
                zs2 = _sb_scores(ch, k_next)
                ws2 = _sb_weights(fetch(t % 2))
                sps2 = _sb_softplus(zs2, None)
                pvs2 = [_dot(w, v_cur) for w in ws2]
                withins2 = _sb_suffix(sps2, tri)
                _sb_accumulate(ch, pvs2, False)
                stash((t + 1) % 2, _sb_log_weights(ch, zs2, sps2, withins2, None))
                return c

            lax.fori_loop(0, base - 1, step, 0)
            _sb_second_half(chains(), fetch((base - 1) % 2), kv(0)[1], False)

        return carry

    lax.fori_loop(0, nblk // G, group_body, 0)
    o_ref[0] = jnp.where(jnp.broadcast_to(lane[:1] < HEAD_DIM, acc0.shape), acc0[...], acc1[...]).astype(BF16)


def stick_breaking(q, k, v):
    bsz, L, _ = q.shape
    nblk = L // SB_TILE
    assert nblk % SB_GROUP == 0
    j = jnp.arange(SB_TILE)
    tri = (j[:, None] > j[None, :]).astype(BF16)
    blk = pl.BlockSpec((1, L, LANES), lambda b, h: (b, 0, h))
    vec = pltpu.VMEM((L, LANES), F32)
    return pl.pallas_call(
        functools.partial(_stick_body, nblk=nblk),
        grid=(bsz, D_HEADS // 2),
        in_specs=[blk, blk, blk, pl.BlockSpec((SB_TILE, SB_TILE), lambda b, h: (0, 0))],
        out_specs=blk,
        out_shape=jax.ShapeDtypeStruct((bsz, L, D_WIDTH), BF16),
        scratch_shapes=[vec, vec, vec, vec,
                        pltpu.VMEM((2, SB_CHAINS, SB_TILE, SB_TILE), F32),
                        pltpu.VMEM((SB_GROUP, 2, SB_TILE, LANES), BF16)],
        compiler_params=_params("arbitrary", "arbitrary"),
        name="stick_breaking",
    )(q, k, v, tri)


def kernel(x, c, ada_w, ada_b, ln_mix_g, ln_ffn_g, ffn_w_in, ffn_w_out, ab_w_in, ab_w_out, a_q_gain, a_k_gain, a_rel_bias, s5_lambda_re, s5_lambda_im, s5_log_dt, s5_b_re, s5_b_im, s5_c_re, s5_c_im, s5_d, s5_w_glu, s5_b_glu, cd_w_in, cd_w_out, ret_norm_g):
    bsz, L, d = x.shape
    mod = ada_modulation(c, ada_w, ada_b)
    mods = [[mod[l, :, None, i * d:(i + 1) * d] for i in range(6)] for l in range(mod.shape[0])]

    sh1, sc1, g1, sh2, sc2, g2 = mods[0]
    q, k, v, u = proj_in0(x, ln_mix_g[0], sh1, sc1, ab_w_in[0], a_q_gain[0], a_k_gain[0])
    a_out = attn_a(q, k, v, a_rel_bias[0])
    ops = s5_operators(s5_lambda_re[0], s5_lambda_im[0], s5_log_dt[0], s5_b_re[0], s5_b_im[0],
                       s5_c_re[0], s5_c_im[0], s5_d[0])
    y_gelu = s5_mixer_gelu(u, ops)
    x = mix_ffn(x, a_out, y_gelu, ab_w_out[0], g1, ln_ffn_g[0], sh2, sc2, g2, ffn_w_in[0], ffn_w_out[0],
                glu=(s5_w_glu[0], s5_b_glu[0]))

    sh1, sc1, g1, sh2, sc2, g2 = mods[1]
    cq, ck, cv, cg, dq, dk, dv = proj_in1(x, ln_mix_g[1], sh1, sc1, cd_w_in[0])
    c_out = retention(cq, ck, cv, cg, ret_norm_g[0])
    d_out = stick_breaking(dq, dk, dv)
    x = mix_ffn(x, c_out, d_out, cd_w_out[0], g1, ln_ffn_g[1], sh2, sc2, g2, ffn_w_in[1], ffn_w_out[1])
    return x
```

```python
import functools
import math

import jax
import jax.numpy as jnp
from jax import lax
from jax.experimental import pallas as pl
from jax.experimental.pallas import tpu as pltpu

F32 = jnp.float32
BF16 = jnp.bfloat16

D_MODEL = 1024
CHUNK = 64
HEAD_DIM = 64
EPS = 1e-6
A_WIDTH = 512
A_HEADS = 8
A_LEFT_CHUNKS = 8
A_BAND = (A_LEFT_CHUNKS + 1) * CHUNK
A_PAD = A_LEFT_CHUNKS * CHUNK
A_MAX_REL = 128
B_WIDTH = 512
B_GROUP = 16
B_GROUPS = 32
B_STATE = 64
C_WIDTH = 512
C_HEADS = 4
C_HEAD_DIM = 128
ROPE_BASE = 10000.0
D_WIDTH = 512
D_HEADS = 8
FF_HIDDEN = 2816
WIDTH = 512
LANES = 128
NEG_BIG = -1e30
A_Q_SCALE = HEAD_DIM ** -0.5 * math.log2(math.e)

VMEM_LIMIT = 56 * 1024 * 1024


def _params(*sem):
    return pltpu.CompilerParams(dimension_semantics=sem, vmem_limit_bytes=VMEM_LIMIT)


def _sigmoid(x):
    return 1.0 / (1.0 + jnp.exp(-x))


def _gelu_tanh(x):
    return 0.5 * x * (1.0 + jnp.tanh(math.sqrt(2.0 / math.pi) * (x + 0.044715 * (x * x * x))))


def _norm_mod(x, g, shift, scale):
    ms = jnp.mean(x * x, axis=-1, keepdims=True)
    y = x * lax.rsqrt(ms + EPS) * g
    return y * (1.0 + scale) + shift


def _dot(a, b):
    return jnp.dot(a, b, preferred_element_type=F32)


def _dot_nt(a, b):
    return lax.dot_general(a, b, (((1,), (1,)), ((), ())), preferred_element_type=F32)


def _ada_body(c_ref, w_ref, b_ref, o_ref):
    c = c_ref[...]
    cond = c * _sigmoid(c)
    o_ref[0] = jnp.dot(cond, w_ref[0], preferred_element_type=F32,
                       precision=lax.Precision.HIGHEST) + b_ref[0]


def ada_modulation(c, ada_w, ada_b):
    depth, d, n = ada_w.shape
    bsz = c.shape[0]
    tn = 1536
    return pl.pallas_call(
        _ada_body,
        grid=(depth, n // tn),
        in_specs=[pl.BlockSpec((bsz, d), lambda l, j: (0, 0)),
                  pl.BlockSpec((1, d, tn), lambda l, j: (l, 0, j)),
                  pl.BlockSpec((1, 1, tn), lambda l, j: (l, 0, j))],
        out_specs=pl.BlockSpec((1, bsz, tn), lambda l, j: (l, 0, j)),
        out_shape=jax.ShapeDtypeStruct((depth, bsz, n), F32),
        compiler_params=_params("arbitrary", "arbitrary"),
        name="ada_modulation",
    )(c, ada_w, ada_b.reshape(depth, 1, n))


def _proj_in0_body(x_ref, g_ref, sh_ref, sc_ref, w_ref, qg_ref, kg_ref, bd_ref, kz_ref, vz_ref,
                   q_ref, k_ref, v_ref, u_ref):
    del kz_ref, vz_ref
    h = _norm_mod(x_ref[0], g_ref[...], sh_ref[0], sc_ref[0]).astype(BF16)

    def head_norm(t, gain):
        ms = _dot((t * t).astype(BF16), bd_ref[...])
        return t * lax.rsqrt(ms + EPS) * gain

    q_ref[0] = head_norm(_dot(h, w_ref[:, 0:WIDTH]), qg_ref[...]).astype(BF16)
    k_ref[0] = head_norm(_dot(h, w_ref[:, WIDTH:2 * WIDTH]), kg_ref[...]).astype(BF16)
    v_ref[0] = _dot(h, w_ref[:, 2 * WIDTH:3 * WIDTH]).astype(BF16)
    u_ref[0] = _dot(h, w_ref[:, 3 * WIDTH:4 * WIDTH]).astype(BF16)


def proj_in0(x, ln_g, shift, scale, w_in, q_gain, k_gain, tm=512):
    bsz, L, d = x.shape
    n = w_in.shape[1]
    assert A_PAD % tm == 0
    head_of = jnp.arange(WIDTH) // HEAD_DIM
    bd = (head_of[:, None] == head_of[None, :]).astype(BF16) * (1.0 / HEAD_DIM)
    qg = (jnp.tile(q_gain.astype(F32), A_HEADS) * A_Q_SCALE).reshape(1, WIDTH)
    kg = jnp.tile(k_gain.astype(F32), A_HEADS).reshape(1, WIDTH)
    row = lambda b, i: (b, i, 0)
    padded_row = lambda b, i: (b, i + A_PAD // tm, 0)
    const2 = lambda b, i: (0, 0)
    per_b = lambda b, i: (b, 0, 0)
    out = jax.ShapeDtypeStruct((bsz, L, WIDTH), BF16)
    padded = jax.ShapeDtypeStruct((bsz, L + A_PAD, WIDTH), BF16)
    zeros = jnp.zeros(padded.shape, BF16)
    hbm = pl.BlockSpec(memory_space=pl.ANY)
    return pl.pallas_call(
        _proj_in0_body,
        grid=(bsz, L // tm),
        in_specs=[pl.BlockSpec((1, tm, d), row),
                  pl.BlockSpec((1, d), const2),
                  pl.BlockSpec((1, 1, d), per_b),
                  pl.BlockSpec((1, 1, d), per_b),
                  pl.BlockSpec((d, n), const2),
                  pl.BlockSpec((1, WIDTH), const2),
                  pl.BlockSpec((1, WIDTH), const2),
                  pl.BlockSpec((WIDTH, WIDTH), const2),
                  hbm, hbm],
        out_specs=[pl.BlockSpec((1, tm, WIDTH), row),
                   pl.BlockSpec((1, tm, WIDTH), padded_row),
                   pl.BlockSpec((1, tm, WIDTH), padded_row),
                   pl.BlockSpec((1, tm, WIDTH), row)],
        out_shape=[out, padded, padded, out],
        input_output_aliases={8: 1, 9: 2},
        compiler_params=_params("arbitrary", "arbitrary"),
        name="proj_in0",
    )(x, ln_g.reshape(1, d), shift, scale, w_in.astype(BF16), qg, kg, bd, zeros, zeros)


def _attn_a_heads(q_ref, k_ref, v_ref, bias_ref, o_ref, start, win, valid):
    tq = q_ref.shape[1]
    lane = lax.broadcasted_iota(jnp.int32, (tq, LANES), 1)
    low = lane < HEAD_DIM
    slices = [slice(hp * LANES, (hp + 1) * LANES) for hp in range(A_HEADS // 2)]
    vws = [v_ref[0, pl.ds(start, win), sl] for sl in slices]
    scores = []
    for hp, sl in enumerate(slices):
        qp = q_ref[0, :, sl]
        kw = k_ref[0, pl.ds(start, win), sl]
        zero = jnp.zeros_like(qp)
        for e in range(2):
            qm = jnp.where(low, qp, zero) if e == 0 else jnp.where(low, zero, qp)
            s = _dot_nt(qm, kw) + bias_ref[2 * hp + e]
            scores.append(s if valid is None else jnp.where(valid, s, NEG_BIG))
    probs, inv_sums = [], []
    for s in scores:
        p = jnp.exp2(s - jnp.max(s, axis=-1, keepdims=True))
        inv_sums.append(1.0 / jnp.sum(p, axis=-1, keepdims=True))
        probs.append(p.astype(BF16))
    outs = [_dot(p, vws[h // 2]) * inv for h, (p, inv) in enumerate(zip(probs, inv_sums))]
    for hp, sl in enumerate(slices):
        o_ref[0, :, sl] = jnp.where(low, outs[2 * hp], outs[2 * hp + 1]).astype(BF16)


def _attn_a_body(q_ref, k_ref, v_ref, bias_ref, o_ref, *, tq):
    j = pl.program_id(1)
    win = tq + A_PAD
    start = pl.multiple_of(j * tq, tq)

    @pl.when(j * tq < A_PAD)
    def _():
        col = lax.broadcasted_iota(jnp.int32, (tq, win), 1)
        valid = col >= A_PAD - j * tq
        _attn_a_heads(q_ref, k_ref, v_ref, bias_ref, o_ref, start, win, valid)

    @pl.when(j * tq >= A_PAD)
    def _():
        _attn_a_heads(q_ref, k_ref, v_ref, bias_ref, o_ref, start, win, None)


def attn_a_bias_table(rel_bias, tq):
    win = tq + A_PAD
    period = tq + win
    m = jnp.arange(period)
    diff = jnp.where(m < win, m, m - period)
    idx = jnp.clip(A_PAD - diff, -A_MAX_REL, A_MAX_REL) + A_MAX_REL
    x = rel_bias.astype(F32)[:, idx] * math.log2(math.e)
    tab = jnp.tile(x, (1, tq + 1))[:, :tq * (period - 1)].reshape(-1, tq, period - 1)[:, :, :win]
    qi = jnp.arange(tq)
    band = jnp.arange(win)[None, :] - (qi[:, None] // CHUNK) * CHUNK
    inside = (band >= 0) & (band < A_BAND)
    return jnp.where(inside[None], tab, NEG_BIG)


def attn_a(q, kp, vp, rel_bias, tq=256):
    bsz, L, _ = q.shape
    win = tq + A_PAD
    bias = attn_a_bias_table(rel_bias, tq)
    per_b = lambda b, i: (b, 0, 0)
    return pl.pallas_call(
        functools.partial(_attn_a_body, tq=tq),
        grid=(bsz, L // tq),
        in_specs=[pl.BlockSpec((1, tq, WIDTH), lambda b, i: (b, i, 0)),
                  pl.BlockSpec((1, L + A_PAD, WIDTH), per_b),
                  pl.BlockSpec((1, L + A_PAD, WIDTH), per_b),
                  pl.BlockSpec((A_HEADS, tq, win), lambda b, i: (0, 0, 0), pipeline_mode=pl.Buffered(1))],
        out_specs=pl.BlockSpec((1, tq, WIDTH), lambda b, i: (b, i, 0)),
        out_shape=jax.ShapeDtypeStruct((bsz, L, WIDTH), BF16),
        compiler_params=_params("arbitrary", "arbitrary"),
        name="attn_a",
    )(q, kp, vp, bias)


S5_LAGS = 72


def _s5_ops_body(lrc_ref, lic_ref, lrr_ref, lir_ref, ldt_ref, btr_ref, bti_ref, ctr_ref, cti_ref, dpad_ref,
                 e1_ref, e2_ref, m_ref, v_ref, w_ref, a_ref):
    hp = lax.Precision.HIGHEST
    kdim = CHUNK * B_GROUP
    dt = jnp.exp(ldt_ref[0])
    n_l = lax.broadcasted_iota(jnp.int32, (B_STATE, LANES), 1).astype(F32)
    mag = jnp.exp(lrc_ref[0] * dt * n_l)
    ang = lic_ref[0] * dt * n_l
    pw_re, pw_im = mag * jnp.cos(ang), mag * jnp.sin(ang)
    lam_re, lam_im = lrr_ref[0], lir_ref[0]
    n_s = lax.broadcasted_iota(jnp.int32, (LANES, B_STATE), 0).astype(F32)
    mag_t = jnp.exp(lam_re * dt * n_s)
    ang_t = lam_im * dt * n_s
    pwt_re, pwt_im = mag_t * jnp.cos(ang_t), mag_t * jnp.sin(ang_t)
    ab_re, ab_im = pwt_re[1:2, :], pwt_im[1:2, :]
    den = lam_re * lam_re + lam_im * lam_im
    f_re = ((ab_re - 1.0) * lam_re + ab_im * lam_im) / den
    f_im = (ab_im * lam_re - (ab_re - 1.0) * lam_im) / den
    bt_re, bt_im = btr_ref[0], bti_ref[0]
    bb_re = f_re * bt_re - f_im * bt_im
    bb_im = f_re * bt_im + f_im * bt_re
    rep = lambda t, e_ref: jnp.dot(t, e_ref[...], preferred_element_type=F32, precision=hp)
    a_re, a_im = rep(pw_re, e1_ref), rep(pw_im, e1_ref)
    c_re, c_im = rep(ctr_ref[0], e2_ref), rep(cti_ref[0], e2_ref)
    ca_re = c_re * a_re - c_im * a_im
    ca_im = c_re * a_im + c_im * a_re
    kern = (jnp.dot(bb_re, ca_re, preferred_element_type=F32, precision=hp)
            - jnp.dot(bb_im, ca_im, preferred_element_type=F32, precision=hp))[:, :kdim]
    row = lax.broadcasted_iota(jnp.int32, (B_GROUP, kdim), 0)
    lane = lax.broadcasted_iota(jnp.int32, (B_GROUP, kdim), 1)
    kern = kern + jnp.where(lane == row, dpad_ref[0], 0.0)
    for s in range(CHUNK):
        shifted = kern if s == 0 else pltpu.roll(kern, s * B_GROUP, 1)
        m_ref[0, s * B_GROUP:(s + 1) * B_GROUP, :] = jnp.where(lane >= s * B_GROUP, shifted, 0.0).astype(BF16)
        r_re = pwt_re[CHUNK - 1 - s:CHUNK - s, :]
        r_im = pwt_im[CHUNK - 1 - s:CHUNK - s, :]
        v_ref[0, s * B_GROUP:(s + 1) * B_GROUP, 0:B_STATE] = (r_re * bb_re - r_im * bb_im).astype(BF16)
        v_ref[0, s * B_GROUP:(s + 1) * B_GROUP, B_STATE:2 * B_STATE] = (r_re * bb_im + r_im * bb_re).astype(BF16)
    width = S5_LAGS * B_GROUP
    w_ref[0, 0:B_STATE, :] = pltpu.roll(ca_re, width - B_GROUP, 1)[:, :kdim].astype(BF16)
    w_ref[0, B_STATE:2 * B_STATE, :] = (-pltpu.roll(ca_im, width - B_GROUP, 1)[:, :kdim]).astype(BF16)
    a64_re, a64_im = pwt_re[CHUNK:CHUNK + 1, :], pwt_im[CHUNK:CHUNK + 1, :]
    a_ref[0, 0:1, 0:B_STATE] = a64_re
    a_ref[0, 0:1, B_STATE:2 * B_STATE] = a64_re
    a_ref[0, 1:2, 0:B_STATE] = -a64_im
    a_ref[0, 1:2, B_STATE:2 * B_STATE] = a64_im


def s5_operators(lam_re, lam_im, log_dt, b_re, b_im, c_re, c_im, d_skip):
    g, p = lam_re.shape
    kdim = CHUNK * B_GROUP
    width = S5_LAGS * B_GROUP
    f = lambda t: t.astype(F32)
    col = jnp.arange(width)
    e1 = (jnp.arange(LANES)[:, None] == col[None, :] // B_GROUP).astype(F32)
    e2 = (jnp.arange(B_GROUP)[:, None] == col[None, :] % B_GROUP).astype(F32)
    dpad = jnp.pad(f(d_skip), ((0, 0), (0, kdim - B_GROUP))).reshape(g, 1, kdim)
    per_g = lambda i: (i, 0, 0)
    const2 = lambda i: (0, 0)
    blk = lambda *shape: pl.BlockSpec((1,) + shape, per_g)
    return pl.pallas_call(
        _s5_ops_body,
        grid=(g,),
        in_specs=[blk(p, 1), blk(p, 1), blk(1, p), blk(1, p), blk(1, 1),
                  blk(B_GROUP, p), blk(B_GROUP, p), blk(p, B_GROUP), blk(p, B_GROUP), blk(1, kdim),
                  pl.BlockSpec((LANES, width), const2), pl.BlockSpec((B_GROUP, width), const2)],
        out_specs=[blk(kdim, kdim), blk(kdim, 2 * p), blk(2 * p, kdim), blk(2, 2 * p)],
        out_shape=[jax.ShapeDtypeStruct((g, kdim, kdim), BF16),
                   jax.ShapeDtypeStruct((g, kdim, 2 * p), BF16),
                   jax.ShapeDtypeStruct((g, 2 * p, kdim), BF16),
                   jax.ShapeDtypeStruct((g, 2, 2 * p), F32)],
        compiler_params=_params("arbitrary"),
        name="s5_operators",
    )(f(lam_re).reshape(g, p, 1), f(lam_im).reshape(g, p, 1), f(lam_re).reshape(g, 1, p), f(lam_im).reshape(g, 1, p),
      f(log_dt).reshape(g, 1, 1), f(b_re).transpose(0, 2, 1), f(b_im).transpose(0, 2, 1),
      f(c_re).transpose(0, 2, 1), f(c_im).transpose(0, 2, 1), dpad, e1, e2)


def _s5_body(u_ref, m_ref, v_ref, w_ref, a_ref, y_ref, xre_ref, xim_ref, *, nc, bsz):
    u = u_ref[0]
    s = _dot(u, v_ref[0])
    s_re, s_im = s[:, 0:B_STATE], s[:, B_STATE:2 * B_STATE]
    a_re = a_ref[0, 0:1, 0:B_STATE]
    a_im = a_ref[0, 1:2, B_STATE:2 * B_STATE]
    x_re = x_im = jnp.zeros((bsz, B_STATE), F32)
    for c in range(nc):
        rows = slice(c * bsz, (c + 1) * bsz)
        xre_ref[rows, :] = x_re
        xim_ref[rows, :] = x_im
        x_re, x_im = (x_re * a_re - x_im * a_im + s_re[rows, :],
                      x_re * a_im + x_im * a_re + s_im[rows, :])
    y = (_dot(u, m_ref[0]) + _dot(xre_ref[...].astype(BF16), w_ref[0, 0:B_STATE, :])
         + _dot(xim_ref[...].astype(BF16), w_ref[0, B_STATE:2 * B_STATE, :]))
    y_ref[0] = _gelu_tanh(y).astype(BF16)


def s5_mixer_gelu(u, ops):
    m, v, w, a = ops
    bsz, L, _ = u.shape
    nc = L // CHUNK
    n = nc * bsz
    kdim = CHUNK * B_GROUP
    ug = u.reshape(bsz, nc, CHUNK, B_GROUPS, B_GROUP).transpose(3, 1, 0, 2, 4).reshape(B_GROUPS, n, kdim)
    per_g = lambda g: (g, 0, 0)
    yg = pl.pallas_call(
        functools.partial(_s5_body, nc=nc, bsz=bsz),
        grid=(B_GROUPS,),
        in_specs=[pl.BlockSpec((1, n, kdim), per_g),
                  pl.BlockSpec((1, kdim, kdim), per_g),
                  pl.BlockSpec((1, kdim, 2 * B_STATE), per_g),
                  pl.BlockSpec((1, 2 * B_STATE, kdim), per_g),
                  pl.BlockSpec((1, 2, 2 * B_STATE), per_g)],
        out_specs=pl.BlockSpec((1, n, kdim), per_g),
        out_shape=jax.ShapeDtypeStruct((B_GROUPS, n, kdim), BF16),
        scratch_shapes=[pltpu.VMEM((n, B_STATE), F32), pltpu.VMEM((n, B_STATE), F32)],
        compiler_params=_params("arbitrary"),
        name="s5_mixer",
    )(ug, m, v, w, a)
    return yg.reshape(B_GROUPS, nc, bsz, CHUNK, B_GROUP).transpose(2, 1, 3, 0, 4).reshape(bsz, L, B_WIDTH)


FFN_CHUNK = 256


def _mix_ffn_body(*refs, glu):
    if glu:
        (x_ref, s1_ref, s2_ref, wglu_ref, bglu_ref, wo_ref, g1_ref, lng_ref, sh_ref, sc_ref, g2_ref,
         wg_ref, wu_ref, wd_ref, o_ref) = refs
        y = s2_ref[0]
        gate = _sigmoid(_dot(y, wglu_ref[...]) + bglu_ref[...])
        s2 = (y.astype(F32) * gate).astype(BF16)
    else:
        (x_ref, s1_ref, s2_ref, wo_ref, g1_ref, lng_ref, sh_ref, sc_ref, g2_ref,
         wg_ref, wu_ref, wd_ref, o_ref) = refs
        s2 = s2_ref[0]
    mix = _dot(s1_ref[0], wo_ref[0:WIDTH, :]) + _dot(s2, wo_ref[WIDTH:2 * WIDTH, :])
    x1 = x_ref[0] + g1_ref[0] * mix
    h = _norm_mod(x1, lng_ref[...], sh_ref[0], sc_ref[0]).astype(BF16)
    acc = jnp.zeros(x1.shape, F32)
    for f in range(FF_HIDDEN // FFN_CHUNK):
        sl = slice(f * FFN_CHUNK, (f + 1) * FFN_CHUNK)
        gt = _dot(h, wg_ref[:, sl])
        up = _dot(h, wu_ref[:, sl])
        act = (gt * _sigmoid(gt) * up).astype(BF16)
        acc = acc + _dot(act, wd_ref[sl, :])
    o_ref[0] = x1 + g2_ref[0] * acc


def mix_ffn(x, s1, s2, w_out, g1, ln_g, shift, scale, g2, w_in, w_down, glu=None, tm=512):
    bsz, L, d = x.shape
    row = lambda b, i: (b, i, 0)
    const2 = lambda b, i: (0, 0)
    per_b = lambda b, i: (b, 0, 0)
    single = pl.Buffered(1)
    weight = lambda shape: pl.BlockSpec(shape, const2, pipeline_mode=single)
    vec = pl.BlockSpec((1, 1, d), per_b)
    w_in = w_in.astype(BF16)
    in_specs = [pl.BlockSpec((1, tm, d), row), pl.BlockSpec((1, tm, WIDTH), row), pl.BlockSpec((1, tm, WIDTH), row)]
    args = [x, s1, s2]
    if glu is not None:
        in_specs += [weight((WIDTH, WIDTH)), pl.BlockSpec((1, WIDTH), const2)]
        args += [glu[0].astype(BF16), glu[1].astype(F32).reshape(1, WIDTH)]
    in_specs += [weight((2 * WIDTH, d)), vec, pl.BlockSpec((1, d), const2), vec, vec, vec,
                 weight((d, FF_HIDDEN)), weight((d, FF_HIDDEN)), weight((FF_HIDDEN, d))]
    args += [w_out.astype(BF16), g1, ln_g.reshape(1, d), shift, scale, g2,
             w_in[:, :FF_HIDDEN], w_in[:, FF_HIDDEN:], w_down.astype(BF16)]
    return pl.pallas_call(
        functools.partial(_mix_ffn_body, glu=glu is not None),
        grid=(bsz, L // tm),
        in_specs=in_specs,
        out_specs=pl.BlockSpec((1, tm, d), row),
        out_shape=jax.ShapeDtypeStruct((bsz, L, d), F32),
        compiler_params=_params("arbitrary", "arbitrary"),
        name="mix_ffn",
    )(*args)


def _rotary(t, cos, sin_signed):
    halves = [pltpu.roll(t[:, h * C_HEAD_DIM:(h + 1) * C_HEAD_DIM], C_HEAD_DIM // 2, 1) for h in range(C_HEADS)]
    return t * cos + jnp.concatenate(halves, axis=1) * sin_signed


def _proj_in1_body(x_ref, g_ref, sh_ref, sc_ref, w_ref, cos_ref, sin_ref,
                   cq_ref, ck_ref, cv_ref, cg_ref, dq_ref, dk_ref, dv_ref):
    h = _norm_mod(x_ref[0], g_ref[...], sh_ref[0], sc_ref[0]).astype(BF16)
    proj = lambda j: _dot(h, w_ref[:, j * WIDTH:(j + 1) * WIDTH])
    cos, sin = cos_ref[...], sin_ref[...]
    cq_ref[0] = _rotary(proj(0), cos, sin).astype(BF16)
    ck_ref[0] = (_rotary(proj(1), cos, sin) * (C_HEAD_DIM ** -0.5)).astype(BF16)
    cv_ref[0] = proj(2).astype(BF16)
    cg_ref[0] = proj(3).astype(BF16)
    dq_ref[0] = (proj(4) * SB_Q_SCALE).astype(BF16)
    dk_ref[0] = proj(5).astype(BF16)
    dv_ref[0] = proj(6).astype(BF16)


def rope_tables(L):
    half = C_HEAD_DIM // 2
    freqs = ROPE_BASE ** (-jnp.arange(half, dtype=F32) / half)
    ang = jnp.arange(L).astype(F32)[:, None] * freqs[None, :]
    cos, sin = jnp.cos(ang), jnp.sin(ang)
    cos_t = jnp.tile(jnp.concatenate([cos, cos], axis=1), (1, C_HEADS))
    sin_t = jnp.tile(jnp.concatenate([-sin, sin], axis=1), (1, C_HEADS))
    return cos_t, sin_t


def proj_in1(x, ln_g, shift, scale, w_in, tm=512):
    bsz, L, d = x.shape
    n = w_in.shape[1]
    cos_t, sin_t = rope_tables(L)
    row = lambda b, i: (b, i, 0)
    const2 = lambda b, i: (0, 0)
    per_b = lambda b, i: (b, 0, 0)
    pos = lambda b, i: (i, 0)
    out = jax.ShapeDtypeStruct((bsz, L, WIDTH), BF16)
    return pl.pallas_call(
        _proj_in1_body,
        grid=(bsz, L // tm),
        in_specs=[pl.BlockSpec((1, tm, d), row),
                  pl.BlockSpec((1, d), const2),
                  pl.BlockSpec((1, 1, d), per_b),
                  pl.BlockSpec((1, 1, d), per_b),
                  pl.BlockSpec((d, n), const2),
                  pl.BlockSpec((tm, WIDTH), pos),
                  pl.BlockSpec((tm, WIDTH), pos)],
        out_specs=[pl.BlockSpec((1, tm, WIDTH), row)] * 7,
        out_shape=[out] * 7,
        compiler_params=_params("arbitrary", "arbitrary"),
        name="proj_in1",
    )(x, ln_g.reshape(1, d), shift, scale, w_in.astype(BF16), cos_t, sin_t)


RET_ROWS = 256
RET_TILE = 512


def _retention_body(q_ref, k_ref, v_ref, g_ref, ng_ref, o_ref, state_ref, dmask_ref, qdec_ref, kdec_ref):
    R = RET_ROWS
    log_gammas = [math.log(1.0 - 2.0 ** (-5.0 - h)) for h in range(C_HEADS)]

    @pl.when((pl.program_id(0) == 0) & (pl.program_id(1) == 0))
    def _tables():
        ri = lax.broadcasted_iota(jnp.int32, (R, R), 0)
        ci = lax.broadcasted_iota(jnp.int32, (R, R), 1)
        rc, cc = ri // CHUNK, ci // CHUNK
        dist = jnp.where(rc == cc, jnp.abs(ri - ci), ri - ci).astype(F32)
        rown = lax.broadcasted_iota(jnp.int32, (R, C_HEAD_DIM), 0).astype(F32)
        for h, lg in enumerate(log_gammas):
            dmask_ref[h] = jnp.where(cc <= rc, jnp.exp(lg * dist), 0.0)
            qdec_ref[h] = jnp.exp(lg * (rown + 1.0))
            kdec_ref[h] = jnp.exp(lg * (R - 1.0 - rown))

    @pl.when(pl.program_id(1) == 0)
    def _reset():
        state_ref[...] = jnp.zeros(state_ref.shape, F32)

    for sub in range(RET_TILE // R):
        rows = slice(sub * R, (sub + 1) * R)
        for h, lg in enumerate(log_gammas):
            sl = slice(h * C_HEAD_DIM, (h + 1) * C_HEAD_DIM)
            q = q_ref[0, rows, sl]
            k = k_ref[0, rows, sl]
            v = v_ref[0, rows, sl]
            state = state_ref[h]
            scores = _dot_nt(q, k) * dmask_ref[h]
            o = (_dot(scores.astype(BF16), v)
                 + _dot((q.astype(F32) * qdec_ref[h]).astype(BF16), state.astype(BF16)))
            kd = (k.astype(F32) * kdec_ref[h]).T.astype(BF16)
            state_ref[h] = state * math.exp(lg * R) + _dot(kd, v)
            mu = jnp.mean(o, axis=-1, keepdims=True)
            cen = o - mu
            var = jnp.mean(cen * cen, axis=-1, keepdims=True)
            on = cen * lax.rsqrt(var + EPS) * ng_ref[:, sl]
            gt = g_ref[0, rows, sl].astype(F32)
            o_ref[0, rows, sl] = (gt * _sigmoid(gt) * on).astype(BF16)


def retention(q, k, v, g, norm_g):
    bsz, L, _ = q.shape
    blk = pl.BlockSpec((1, RET_TILE, C_WIDTH), lambda b, i: (b, i, 0))
    tab = pltpu.VMEM((C_HEADS, RET_ROWS, C_HEAD_DIM), F32)
    return pl.pallas_call(
        _retention_body,
        grid=(bsz, L // RET_TILE),
        in_specs=[blk, blk, blk, blk, pl.BlockSpec((1, C_WIDTH), lambda b, i: (0, 0))],
        out_specs=blk,
        out_shape=jax.ShapeDtypeStruct((bsz, L, C_WIDTH), BF16),
        scratch_shapes=[pltpu.VMEM((C_HEADS, C_HEAD_DIM, C_HEAD_DIM), F32),
                        pltpu.VMEM((C_HEADS, RET_ROWS, RET_ROWS), F32), tab, tab],
        compiler_params=_params("arbitrary", "arbitrary"),
        name="retention",
    )(q, k, v, g, norm_g.astype(F32).reshape(1, C_WIDTH))


SB_TILE = 256
SB_Q_SCALE = HEAD_DIM ** -0.5 * math.log2(math.e)
SB_GROUP = 4
SB_CHAINS = 2 * SB_GROUP


def _sb_scores(chains, kblk):
    return [_dot_nt(qm, kblk) for qm, _, _, _ in chains]


def _sb_softplus(zs, causal):
    sps = []
    for z in zs:
        sp = jnp.maximum(z, 0.0) + jnp.log2(1.0 + jnp.exp2(-jnp.abs(z)))
        sps.append(sp if causal is None else jnp.where(causal, sp, 0.0))
    return sps


def _sb_suffix(sps, tri):
    return [_dot(sp.astype(BF16), tri) for sp in sps]


def _sb_log_weights(chains, zs, sps, withins, causal):
    out = []
    for (_, _, lat_ref, r0), z, sp, within in zip(chains, zs, sps, withins):
        rows = pl.ds(r0, SB_TILE)
        row_sum = jnp.sum(sp, axis=-1, keepdims=True)
        if causal is None:
            later = lat_ref[rows, :]
            out.append(z - sp - within - jnp.concatenate([later, later], axis=1))
            lat_ref[rows, :] = later + row_sum
        else:
            out.append(jnp.where(causal, z - sp - within, NEG_BIG))
            lat_ref[rows, :] = jnp.broadcast_to(row_sum, (SB_TILE, LANES))
    return out


def _sb_first_half(chains, kblk, tri, causal):
    zs = _sb_scores(chains, kblk)
    sps = _sb_softplus(zs, causal)
    return _sb_log_weights(chains, zs, sps, _sb_suffix(sps, tri), causal)


def _sb_weights(logws):
    return [jnp.exp2(t).astype(BF16) for t in logws]


def _sb_accumulate(chains, pvs, first):
    for (_, acc_ref, _, r0), pv in zip(chains, pvs):
        rows = pl.ds(r0, SB_TILE)
        acc_ref[rows, :] = pv if first else acc_ref[rows, :] + pv


def _sb_second_half(chains, logws, vblk, first):
    _sb_accumulate(chains, [_dot(w, vblk) for w in _sb_weights(logws)], first)


def _stick_body(q_ref, k_ref, v_ref, tri_ref, o_ref, acc0, acc1, lat0, lat1, logw_ref, qm_ref, *, nblk):
    T, G = SB_TILE, SB_GROUP
    accs, lats = (acc0, acc1), (lat0, lat1)
    lane = lax.broadcasted_iota(jnp.int32, (T, LANES), 1)
    low = lane < HEAD_DIM
    causal = lax.broadcasted_iota(jnp.int32, (T, T), 1) < lax.broadcasted_iota(jnp.int32, (T, T), 0)
    tri = tri_ref[...]

    def group_body(g, carry):
        base = g * G
        rows = [pl.ds(pl.multiple_of((base + a) * T, T), T) for a in range(G)]
        for a in range(G):
            qp = q_ref[0, rows[a], :]
            zero = jnp.zeros_like(qp)
            qm_ref[a, 0] = jnp.where(low, qp, zero)
            qm_ref[a, 1] = jnp.where(low, zero, qp)

        kblks = [k_ref[0, rows[b], :] for b in range(G)]
        vblks = [v_ref[0, rows[b], :] for b in range(G)]
        tiles = [(a, b, e) for b in reversed(range(G)) for a in range(b, G) for e in range(2)]
        zs = {c: _dot_nt(qm_ref[c[0], c[2]], kblks[c[1]]) for c in tiles}
        sps = {c: _sb_softplus([zs[c]], causal if c[0] == c[1] else None)[0] for c in tiles}
        withins = {c: _dot(sps[c].astype(BF16), tri) for c in tiles}
        later, logws = {}, {}
        for c in tiles:
            a, b, e = c
            row_sum = jnp.sum(sps[c], axis=-1, keepdims=True)
            if a == b:
                logws[c] = jnp.where(causal, zs[c] - sps[c] - withins[c], NEG_BIG)
                later[a, e] = jnp.broadcast_to(row_sum, (T, LANES))
            else:
                lt = later[a, e]
                logws[c] = zs[c] - sps[c] - withins[c] - jnp.concatenate([lt, lt], axis=1)
                later[a, e] = lt + row_sum
        ws = {c: jnp.exp2(logws[c]).astype(BF16) for c in tiles}
        pvs = {c: _dot(ws[c], vblks[c[1]]) for c in tiles}
        for a in range(G):
            for e in range(2):
                accs[e][rows[a], :] = functools.reduce(lambda x, y: x + y, [pvs[a, b, e] for b in range(a, -1, -1)])
                lats[e][rows[a], :] = later[a, e]

        def chains():
            return [(qm_ref[a, e], accs[e], lats[e], pl.multiple_of((base + a) * T, T))
                    for a in range(G) for e in range(2)]

        def kv(kb):
            r = pl.ds(pl.multiple_of(kb * T, T), T)
            return k_ref[0, r, :], v_ref[0, r, :]

        def stash(slot, vals):
            for c, t in enumerate(vals):
                logw_ref[slot, c] = t

        def fetch(slot):
            return [logw_ref[slot, c] for c in range(SB_CHAINS)]

        @pl.when(base > 0)
        def _earlier():
            stash(0, _sb_first_half(chains(), kv(base - 1)[0], tri, None))

            def step(t, c):
                ch = chains()
                _, v_cur = kv(base - 1 - t)
                k_next, _ = kv(base - 2 - t)
                zs2 = _sb_scores(ch, k_next)
                ws2 = _sb_weights(fetch(t % 2))
                sps2 = _sb_softplus(zs2, None)
                pvs2 = [_dot(w, v_cur) for w in ws2]
                withins2 = _sb_suffix(sps2, tri)
                _sb_accumulate(ch, pvs2, False)
                stash((t + 1) % 2, _sb_log_weights(ch, zs2, sps2, withins2, None))
                return c

            lax.fori_loop(0, base - 1, step, 0)
            _sb_second_half(chains(), fetch((base - 1) % 2), kv(0)[1], False)

        return carry

    lax.fori_loop(0, nblk // G, group_body, 0)
    o_ref[0] = jnp.where(jnp.broadcast_to(lane[:1] < HEAD_DIM, acc0.shape), acc0[...], acc1[...]).astype(BF16)


def stick_breaking(q, k, v):
    bsz, L, _ = q.shape
    nblk = L // SB_TILE
    assert nblk % SB_GROUP == 0
    j = jnp.arange(SB_TILE)
    tri = (j[:, None] > j[None, :]).astype(BF16)
    blk = pl.BlockSpec((1, L, LANES), lambda b, h: (b, 0, h))
    vec = pltpu.VMEM((L, LANES), F32)
    return pl.pallas_call(
        functools.partial(_stick_body, nblk=nblk),
        grid=(bsz, D_HEADS // 2),
        in_specs=[blk, blk, blk, pl.BlockSpec((SB_TILE, SB_TILE), lambda b, h: (0, 0))],
        out_specs=blk,
        out_shape=jax.ShapeDtypeStruct((bsz, L, D_WIDTH), BF16),
        scratch_shapes=[vec, vec, vec, vec,
                        pltpu.VMEM((2, SB_CHAINS, SB_TILE, SB_TILE), F32),
                        pltpu.VMEM((SB_GROUP, 2, SB_TILE, LANES), BF16)],
        compiler_params=_params("arbitrary", "arbitrary"),
        name="stick_breaking",
    )(q, k, v, tri)


def kernel(x, c, ada_w, ada_b, ln_mix_g, ln_ffn_g, ffn_w_in, ffn_w_out, ab_w_in, ab_w_out, a_q_gain, a_k_gain, a_rel_bias, s5_lambda_re, s5_lambda_im, s5_log_dt, s5_b_re, s5_b_im, s5_c_re, s5_c_im, s5_d, s5_w_glu, s5_b_glu, cd_w_in, cd_w_out, ret_norm_g):
    bsz, L, d = x.shape
    mod = ada_modulation(c, ada_w, ada_b)
    mods = [[mod[l, :, None, i * d:(i + 1) * d] for i in range(6)] for l in range(mod.shape[0])]

    sh1, sc1, g1, sh2, sc2, g2 = mods[0]
    q, k, v, u = proj_in0(x, ln_mix_g[0], sh1, sc1, ab_w_in[0], a_q_gain[0], a_k_gain[0])
    a_out = attn_a(q, k, v, a_rel_bias[0])
    ops = s5_operators(s5_lambda_re[0], s5_lambda_im[0], s5_log_dt[0], s5_b_re[0], s5_b_im[0],
                       s5_c_re[0], s5_c_im[0], s5_d[0])
    y_gelu = s5_mixer_gelu(u, ops)
    x = mix_ffn(x, a_out, y_gelu, ab_w_out[0], g1, ln_ffn_g[0], sh2, sc2, g2, ffn_w_in[0], ffn_w_out[0],
                glu=(s5_w_glu[0], s5_b_glu[0]))

    sh1, sc1, g1, sh2, sc2, g2 = mods[1]
    cq, ck, cv, cg, dq, dk, dv = proj_in1(x, ln_mix_g[1], sh1, sc1, cd_w_in[0])
    c_out = retention(cq, ck, cv, cg, ret_norm_g[0])
    d_out = stick_breaking(dq, dk, dv)
    x = mix_ffn(x, c_out, d_out, cd_w_out[0], g1, ln_ffn_g[1], sh2, sc2, g2, ffn_w_in[1], ffn_w_out[1])
    return x
```
